```python
import math
import jax
import jax.numpy as jnp
from jax import lax
import numpy as np

D_MODEL = 4096
BATCH = 1
SEQ = 16384
DEPTH = 4

GRID_W = 64
CTX_LEN = 256
N_MIXERS = 3
MIX_HYENA = 0
MIX_GQA = 1
MIX_DIFF = 2
N_MOD = 6
ADA_RANK = 256
D_FF = (((8 * D_MODEL + 2) // 3 + 255) // 256) * 256
HEAD_DIM = 128
GQA_Q_HEADS = D_MODEL // HEAD_DIM
GQA_KV_HEADS = GQA_Q_HEADS // 4
GQA_GROUP = GQA_Q_HEADS // GQA_KV_HEADS
DIFF_HEADS = D_MODEL // (2 * HEAD_DIM)
ROPE_THETA = 10000.0
Q_BLOCK = 128
NORM_EPS = 1e-6
HY_ORDER = 2
HY_EMB_DIM = 33
HY_FILTER_WIDTH = 64
HY_SHORT_CONV = 3
HY_DECAY_TARGET = 1e-2
HY_FAST_DECAY_PCT = 0.3
HY_SLOW_DECAY_PCT = 1.5

kernel_name = "hybrid_hyena_gqa_diffattn_dit"


def rms_norm(x, gain):
    x32 = x.astype(jnp.float32)
    y = x32 * lax.rsqrt(jnp.mean(x32 * x32, axis=-1, keepdims=True) + NORM_EPS)
    return (y * gain.astype(jnp.float32)).astype(x.dtype)


def modulate(h, shift, scale):
    return h * (1 + scale) + shift


def ada_modulation(cond, a, b, bias):
    m = (jax.nn.silu(cond) @ a) @ b + bias
    return jnp.split(m, N_MOD, axis=-1)


def swiglu(h, w_gate, w_up, w_down):
    return (jax.nn.silu(h @ w_gate) * (h @ w_up)) @ w_down


def axial_rope(n_tok):
    rows = n_tok // GRID_W
    row = jnp.repeat(jnp.arange(rows, dtype=jnp.float32), GRID_W)
    col = (jnp.arange(rows * GRID_W) % GRID_W).astype(jnp.float32)
    n_pair = HEAD_DIM // 4
    inv_freq = ROPE_THETA ** (-jnp.arange(n_pair, dtype=jnp.float32) / n_pair)
    ang = jnp.concatenate([row[:, None] * inv_freq, col[:, None] * inv_freq], axis=-1)
    return jnp.cos(ang), jnp.sin(ang)


def apply_rope(x, cos, sin):
    n_tok = x.shape[1]
    bshape = (n_tok,) + (1,) * (x.ndim - 3) + (HEAD_DIM // 2,)
    cs = cos.reshape(bshape)
    sn = sin.reshape(bshape)
    xp = x.astype(jnp.float32).reshape(x.shape[:-1] + (HEAD_DIM // 2, 2))
    x1, x2 = xp[..., 0], xp[..., 1]
    out = jnp.stack([x1 * cs - x2 * sn, x1 * sn + x2 * cs], axis=-1)
    return out.reshape(x.shape).astype(x.dtype)


def sweep_query_blocks(q, fn):
    b, s = q.shape[:2]
    nb = s // Q_BLOCK
    qb = jnp.moveaxis(q.reshape((b, nb, Q_BLOCK) + q.shape[2:]), 1, 0)
    out = jnp.moveaxis(lax.map(fn, qb), 0, 1)
    return out.reshape((b, s) + out.shape[3:])


def gqa_attend(q, k, v):
    s = jnp.einsum("bqhgd,bkhd->bhgqk", q, k).astype(jnp.float32) * (HEAD_DIM ** -0.5)
    p = jax.nn.softmax(s, axis=-1).astype(v.dtype)
    return jnp.einsum("bhgqk,bkhd->bqhgd", p, v)


def diff_attend(q, k, v, lam):
    s = jnp.einsum("bqhmd,bkhmd->bhmqk", q, k).astype(jnp.float32) * (HEAD_DIM ** -0.5)
    p = jax.nn.softmax(s, axis=-1)
    a = (p[:, :, 0] - lam * p[:, :, 1]).astype(v.dtype)
    return jnp.einsum("bhqk,bkhe->bqhe", a, v)


def gqa_mixer(h, hc, ctx_out, w_q, w_k, w_v, q_gain, k_gain, w_o, cos, sin):
    b, n_tok, _ = h.shape
    n_ctx = hc.shape[1]

    def queries(u):
        return rms_norm((u @ w_q).reshape(u.shape[0], u.shape[1], GQA_Q_HEADS, HEAD_DIM), q_gain)

    def keys_values(u):
        k = rms_norm((u @ w_k).reshape(u.shape[0], u.shape[1], GQA_KV_HEADS, HEAD_DIM), k_gain)
        v = (u @ w_v).reshape(u.shape[0], u.shape[1], GQA_KV_HEADS, HEAD_DIM)
        return k, v

    q = apply_rope(queries(h), cos, sin).reshape(b, n_tok, GQA_KV_HEADS, GQA_GROUP, HEAD_DIM)
    k, v = keys_values(h)
    k = apply_rope(k, cos, sin)
    kc, vc = keys_values(hc)
    k_all = jnp.concatenate([k, kc], axis=1)
    v_all = jnp.concatenate([v, vc], axis=1)
    o = sweep_query_blocks(q, lambda qb: gqa_attend(qb, k_all, v_all))
    y = o.reshape(b, n_tok, D_MODEL) @ w_o
    if not ctx_out:
        return y, None
    qc = queries(hc).reshape(b, n_ctx, GQA_KV_HEADS, GQA_GROUP, HEAD_DIM)
    yc = gqa_attend(qc, kc, vc).reshape(b, n_ctx, D_MODEL) @ w_o
    return y, yc


def diff_mixer(h, hc, ctx_out, layer_idx, w_q, w_k, w_v, q_gain, k_gain,
               lq1, lk1, lq2, lk2, sub_gain, w_o, cos, sin):
    lam_init = 0.8 - 0.6 * math.exp(-0.3 * layer_idx)
    lam = (jnp.exp(jnp.sum(lq1.astype(jnp.float32) * lk1.astype(jnp.float32)))
           - jnp.exp(jnp.sum(lq2.astype(jnp.float32) * lk2.astype(jnp.float32))) + lam_init)

    def qk(u, w, gain):
        return rms_norm((u @ w).reshape(u.shape[0], u.shape[1], DIFF_HEADS, 2, HEAD_DIM), gain)

    def val(u):
        return (u @ w_v).reshape(u.shape[0], u.shape[1], DIFF_HEADS, 2 * HEAD_DIM)

    def finish(o):
        o = rms_norm(o, sub_gain) * (1.0 - lam_init)
        return o.reshape(o.shape[0], o.shape[1], D_MODEL) @ w_o

    q = apply_rope(qk(h, w_q, q_gain), cos, sin)
    k = apply_rope(qk(h, w_k, k_gain), cos, sin)
    kc = qk(hc, w_k, k_gain)
    vc = val(hc)
    k_all = jnp.concatenate([k, kc], axis=1)
    v_all = jnp.concatenate([val(h), vc], axis=1)
    y = finish(sweep_query_blocks(q, lambda qb: diff_attend(qb, k_all, v_all, lam)))
    if not ctx_out:
        return y, None
    yc = finish(diff_attend(qk(hc, w_q, q_gain), kc, vc, lam))
    return y, yc


def hyena_filter_spectrum(n_tok, f_w1, f_b1, f_w2, f_b2, f_w3, f_freq):
    t = jnp.linspace(0.0, 1.0, n_tok, dtype=jnp.float32)[:, None]
    bands = (HY_EMB_DIM - 1) // 2
    w = (2.0 * math.pi / n_tok) * jnp.arange(n_tok, dtype=jnp.float32)[:, None]
    f = jnp.linspace(1e-4, bands - 1, bands, dtype=jnp.float32)[None, :]
    feats = jnp.concatenate([t, jnp.cos(f * w), -jnp.sin(f * w)], axis=-1)
    g = jnp.sin(f_freq[0] * (feats @ f_w1 + f_b1))
    g = jnp.sin(f_freq[1] * (g @ f_w2 + f_b2))
    filt = (g @ f_w3).astype(jnp.float32).reshape(n_tok, 2, HY_ORDER, D_MODEL)
    max_decay = math.log(HY_DECAY_TARGET) / HY_FAST_DECAY_PCT
    min_decay = math.log(HY_DECAY_TARGET) / HY_SLOW_DECAY_PCT
    deltas = jnp.abs(jnp.linspace(min_decay, max_decay, D_MODEL, dtype=jnp.float32))
    filt = filt * jnp.exp(-t * deltas)[:, None, None, :]
    fwd, bwd = filt[:, 0], filt[:, 1]
    two_sided = jnp.concatenate(
        [fwd, jnp.zeros((1, HY_ORDER, D_MODEL), jnp.float32), bwd[:0:-1]], axis=0)
    two_sided = two_sided / (jnp.sum(jnp.abs(two_sided), axis=0, keepdims=True) + NORM_EPS)
    return jnp.fft.rfft(two_sided, axis=0)


def short_conv(z, w, b):
    n_tok = z.shape[1]
    pad = HY_SHORT_CONV // 2
    zp = jnp.pad(z, ((0, 0), (pad, pad), (0, 0)))
    out = b
    for j in range(HY_SHORT_CONV):
        out = out + zp[:, j:j + n_tok] * w[j]
    return out


def hyena_mixer(u, w_in, b_in, conv_w, conv_b, f_w1, f_b1, f_w2, f_b2, f_w3, f_freq, f_bias, w_out):
    n_tok = u.shape[1]
    z = short_conv(u @ w_in + b_in, conv_w, conv_b)
    v, *gates = jnp.split(z, HY_ORDER + 1, axis=-1)
    spec = hyena_filter_spectrum(n_tok, f_w1, f_b1, f_w2, f_b2, f_w3, f_freq)
    y = v.astype(jnp.float32)
    for n, gate in enumerate(gates):
        conv = jnp.fft.irfft(jnp.fft.rfft(y, n=2 * n_tok, axis=1) * spec[None, :, n],
                             n=2 * n_tok, axis=1)[:, :n_tok]
        y = gate.astype(jnp.float32) * (conv + y * f_bias[n].astype(jnp.float32))
    return y.astype(u.dtype) @ w_out


def setup_inputs(seed: int = 0) -> dict:
    key = jax.random.key(seed)
    keys = iter(jax.random.split(key, 48))
    n_hy = len(range(MIX_HYENA, DEPTH, N_MIXERS))
    n_gqa = len(range(MIX_GQA, DEPTH, N_MIXERS))
    n_diff = len(range(MIX_DIFF, DEPTH, N_MIXERS))
    d = D_MODEL

    def normal(shape, scale):
        return jax.random.normal(next(keys), shape, jnp.float32) * scale

    def gain(shape):
        return 1.0 + 0.05 * jax.random.normal(next(keys), shape, jnp.float32)

    return {
        "x": normal((BATCH, SEQ, d), 1.0),
        "c": normal((BATCH, d), 1.0),
        "ctx": normal((BATCH, CTX_LEN, d), 1.0),
        "c_ctx": normal((d,), 1.0),
        "ada_a": normal((DEPTH, d, ADA_RANK), d ** -0.5),
        "ada_b": normal((DEPTH, ADA_RANK, N_MOD * d), 0.5 * ADA_RANK ** -0.5),
        "ada_bias": normal((DEPTH, N_MOD * d), 0.02),
        "norm1_g": gain((DEPTH, d)),
        "norm2_g": gain((DEPTH, d)),
        "ffn_w_gate": normal((DEPTH, d, D_FF), d ** -0.5),
        "ffn_w_up": normal((DEPTH, d, D_FF), d ** -0.5),
        "ffn_w_down": normal((DEPTH, D_FF, d), D_FF ** -0.5),
        "hy_w_in": normal((n_hy, d, (HY_ORDER + 1) * d), d ** -0.5),
        "hy_b_in": normal((n_hy, (HY_ORDER + 1) * d), 0.02),
        "hy_conv_w": normal((n_hy, HY_SHORT_CONV, (HY_ORDER + 1) * d), HY_SHORT_CONV ** -0.5),
        "hy_conv_b": normal((n_hy, (HY_ORDER + 1) * d), 0.02),
        "hy_f_w1": normal((n_hy, HY_EMB_DIM, HY_FILTER_WIDTH), HY_EMB_DIM ** -0.5),
        "hy_f_b1": normal((n_hy, HY_FILTER_WIDTH), 0.02),
        "hy_f_w2": normal((n_hy, HY_FILTER_WIDTH, HY_FILTER_WIDTH), HY_FILTER_WIDTH ** -0.5),
        "hy_f_b2": normal((n_hy, HY_FILTER_WIDTH), 0.02),
        "hy_f_w3": normal((n_hy, HY_FILTER_WIDTH, 2 * HY_ORDER * d), HY_FILTER_WIDTH ** -0.5),
        "hy_f_freq": gain((n_hy, 2, HY_FILTER_WIDTH)),
        "hy_f_bias": normal((n_hy, HY_ORDER, d), 1.0),
        "hy_w_out": normal((n_hy, d, d), d ** -0.5),
        "gqa_w_q": normal((n_gqa, d, GQA_Q_HEADS * HEAD_DIM), d ** -0.5),
        "gqa_w_k": normal((n_gqa, d, GQA_KV_HEADS * HEAD_DIM), d ** -0.5),
        "gqa_w_v": normal((n_gqa, d, GQA_KV_HEADS * HEAD_DIM), d ** -0.5),
        "gqa_q_norm": gain((n_gqa, HEAD_DIM)),
        "gqa_k_norm": gain((n_gqa, HEAD_DIM)),
        "gqa_w_o": normal((n_gqa, GQA_Q_HEADS * HEAD_DIM, d), (GQA_Q_HEADS * HEAD_DIM) ** -0.5),
        "diff_w_q": normal((n_diff, d, d), d ** -0.5),
        "diff_w_k": normal((n_diff, d, d), d ** -0.5),
        "diff_w_v": normal((n_diff, d, d), d ** -0.5),
        "diff_q_norm": gain((n_diff, HEAD_DIM)),
        "diff_k_norm": gain((n_diff, HEAD_DIM)),
        "diff_lambda_q1": normal((n_diff, HEAD_DIM), 0.1),
        "diff_lambda_k1": normal((n_diff, HEAD_DIM), 0.1),
        "diff_lambda_q2": normal((n_diff, HEAD_DIM), 0.1),
        "diff_lambda_k2": normal((n_diff, HEAD_DIM), 0.1),
        "diff_sub_norm": gain((n_diff, 2 * HEAD_DIM)),
        "diff_w_o": normal((n_diff, d, d), d ** -0.5),
    }


def reference(x, c, ctx, c_ctx, ada_a, ada_b, ada_bias, norm1_g, norm2_g,
              ffn_w_gate, ffn_w_up, ffn_w_down,
              hy_w_in, hy_b_in, hy_conv_w, hy_conv_b, hy_f_w1, hy_f_b1, hy_f_w2, hy_f_b2,
              hy_f_w3, hy_f_freq, hy_f_bias, hy_w_out,
              gqa_w_q, gqa_w_k, gqa_w_v, gqa_q_norm, gqa_k_norm, gqa_w_o,
              diff_w_q, diff_w_k, diff_w_v, diff_q_norm, diff_k_norm,
              diff_lambda_q1, diff_lambda_k1, diff_lambda_q2, diff_lambda_k2,
              diff_sub_norm, diff_w_o):
    n_tok = x.shape[1]
    cos, sin = axial_rope(n_tok)
    kinds = [i % N_MIXERS for i in range(DEPTH)]
    for i in range(DEPTH):
        kind = kinds[i]
        j = i // N_MIXERS
        reads_ctx = kind != MIX_HYENA
        update_ctx = any(kk != MIX_HYENA for kk in kinds[i + 1:])
        sh1, sc1, g1, sh2, sc2, g2 = ada_modulation(c[:, None, :], ada_a[i], ada_b[i], ada_bias[i])
        h = modulate(rms_norm(x, norm1_g[i]), sh1, sc1)
        if reads_ctx or update_ctx:
            csh1, csc1, cg1, csh2, csc2, cg2 = ada_modulation(c_ctx, ada_a[i], ada_b[i], ada_bias[i])
            hc = modulate(rms_norm(ctx, norm1_g[i]), csh1, csc1)
        if kind == MIX_HYENA:
            hy = (hy_w_in[j], hy_b_in[j], hy_conv_w[j], hy_conv_b[j], hy_f_w1[j], hy_f_b1[j],
                  hy_f_w2[j], hy_f_b2[j], hy_f_w3[j], hy_f_freq[j], hy_f_bias[j], hy_w_out[j])
            y = hyena_mixer(h, *hy)
            yc = hyena_mixer(hc, *hy) if update_ctx else None
        elif kind == MIX_GQA:
            y, yc = gqa_mixer(h, hc, update_ctx, gqa_w_q[j], gqa_w_k[j], gqa_w_v[j],
                              gqa_q_norm[j], gqa_k_norm[j], gqa_w_o[j], cos, sin)
        else:
            y, yc = diff_mixer(h, hc, update_ctx, i, diff_w_q[j], diff_w_k[j], diff_w_v[j],
                               diff_q_norm[j], diff_k_norm[j], diff_lambda_q1[j], diff_lambda_k1[j],
                               diff_lambda_q2[j], diff_lambda_k2[j], diff_sub_norm[j], diff_w_o[j],
                               cos, sin)
        x = x + g1 * y
        x = x + g2 * swiglu(modulate(rms_norm(x, norm2_g[i]), sh2, sc2),
                            ffn_w_gate[i], ffn_w_up[i], ffn_w_down[i])
        if update_ctx:
            ctx = ctx + cg1 * yc
            ctx = ctx + cg2 * swiglu(modulate(rms_norm(ctx, norm2_g[i]), csh2, csc2),
                                     ffn_w_gate[i], ffn_w_up[i], ffn_w_down[i])
    return x
```

```python
import functools
import math

import numpy as np
import jax
import jax.numpy as jnp
from jax import lax
from jax.experimental import pallas as pl
from jax.experimental.pallas import tpu as pltpu

HEAD_DIM = 128
GRID_W = 64
GQA_GROUP = 4
N_MOD = 6
N_MIXERS = 3
MIX_HYENA, MIX_GQA, MIX_DIFF = 0, 1, 2
NORM_EPS = 1e-6
ROPE_THETA = 10000.0
HY_ORDER = 2
HY_EMB_DIM = 33
HY_DECAY_TARGET = 1e-2
HY_FAST_DECAY_PCT = 0.3
HY_SLOW_DECAY_PCT = 1.5

V7X_VMEM_BYTES = 64 * 1024 * 1024
VMEM_LIMIT_BYTES = V7X_VMEM_BYTES - 8 * 1024 * 1024
LANES = 128
SUBLANES = 8
BF16_ROWS = 16

F32 = jnp.float32
BF16 = jnp.bfloat16
HIGHEST = lax.Precision.HIGHEST


def _pick(dim, pref, align):
    if dim <= pref:
        return dim
    best = None
    for d in range(align, pref + 1, align):
        if dim % d == 0:
            best = d
    assert best is not None, (dim, pref, align)
    return best


def _params(*sem):
    return pltpu.CompilerParams(dimension_semantics=sem, vmem_limit_bytes=VMEM_LIMIT_BYTES)


def _silu(x):
    return x * (1.0 / (1.0 + jnp.exp(-x)))


def _ada_kernel(cc_ref, a_ref, b_ref, bias_ref, o_ref):
    s = _silu(cc_ref[...])
    t = jnp.dot(s, a_ref[0], precision=HIGHEST, preferred_element_type=F32)
    o_ref[0] = jnp.dot(t, b_ref[0], precision=HIGHEST, preferred_element_type=F32) + bias_ref[0]


def _ada_mods(cc, ada_a, ada_b, ada_bias):
    depth, d, r = ada_a.shape
    n = ada_b.shape[2]
    tn = _pick(n, 2048, LANES)
    return pl.pallas_call(
        _ada_kernel,
        grid=(depth, n // tn),
        in_specs=[
            pl.BlockSpec((SUBLANES, d), lambda l, j: (0, 0)),
            pl.BlockSpec((1, d, r), lambda l, j: (l, 0, 0)),
            pl.BlockSpec((1, r, tn), lambda l, j: (l, 0, j)),
            pl.BlockSpec((1, 1, tn), lambda l, j: (l, 0, j)),
        ],
        out_specs=pl.BlockSpec((1, SUBLANES, tn), lambda l, j: (l, 0, j)),
        out_shape=jax.ShapeDtypeStruct((depth, SUBLANES, n), F32),
        compiler_params=_params("parallel", "parallel"),
        name="ada_mods",
    )(cc, ada_a, ada_b, ada_bias.reshape(depth, 1, n))


def _norm_mod_kernel(x_ref, g_ref, sh_ref, sc_ref, o_ref):
    x = x_ref[...]
    ms = jnp.mean(x * x, axis=-1, keepdims=True)
    y = x * lax.rsqrt(ms + NORM_EPS) * g_ref[...]
    o_ref[...] = (y * (1.0 + sc_ref[...]) + sh_ref[...]).astype(o_ref.dtype)


def _norm_mod(x, gain, shift, scale):
    m, d = x.shape
    tr = _pick(m, 256, SUBLANES)
    vec = pl.BlockSpec((1, d), lambda i: (0, 0))
    return pl.pallas_call(
        _norm_mod_kernel,
        grid=(m // tr,),
        in_specs=[pl.BlockSpec((tr, d), lambda i: (i, 0)), vec, vec, vec],
        out_specs=pl.BlockSpec((tr, d), lambda i: (i, 0)),
        out_shape=jax.ShapeDtypeStruct((m, d), BF16),
        compiler_params=_params("parallel"),
        name="norm_mod",
    )(x, gain.reshape(1, d), shift.reshape(1, d), scale.reshape(1, d))


def _head_norm_rope(y, gain, cos, sin):
    outs = []
    for h in range(y.shape[1] // HEAD_DIM):
        t = y[:, h * HEAD_DIM:(h + 1) * HEAD_DIM]
        ms = jnp.mean(t * t, axis=-1, keepdims=True)
        t = t * lax.rsqrt(ms + NORM_EPS) * gain
        if cos is not None:
            t = t * cos + pltpu.roll(t, HEAD_DIM // 2, axis=1) * sin
        outs.append(t)
    return outs[0] if len(outs) == 1 else jnp.concatenate(outs, axis=1)


def _mm_kernel(*refs, mode, nk):
    n_w = 2 if mode == "swiglu" else 1
    a_ref, w_refs = refs[0], refs[1:1 + n_w]
    n_extra = {"bias": 1, "plain": 0, "swiglu": 0, "resid": 2, "headnorm": 1, "headrope": 3}[mode]
    extra = refs[1 + n_w:1 + n_w + n_extra]
    o_ref = refs[1 + n_w + n_extra]
    accs = refs[2 + n_w + n_extra:]
    k = pl.program_id(2)

    a = a_ref[...]
    parts = [jnp.dot(a, w[...], preferred_element_type=F32) for w in w_refs]

    def epilogue(ys):
        y = ys[0]
        if mode == "bias":
            y = y + extra[0][...]
        elif mode == "swiglu":
            y = _silu(y) * ys[1]
        elif mode == "resid":
            y = extra[0][...] + extra[1][...] * y
        elif mode == "headnorm":
            y = _head_norm_rope(y, extra[0][...], None, None)
        elif mode == "headrope":
            y = _head_norm_rope(y, extra[0][...], extra[1][...], extra[2][...])
        o_ref[...] = y.astype(o_ref.dtype)

    if nk == 1:
        epilogue(parts)
        return

    @pl.when(k == 0)
    def _():
        for acc, p in zip(accs, parts):
            acc[...] = p

    @pl.when(jnp.logical_and(k > 0, k < nk - 1))
    def _():
        for acc, p in zip(accs, parts):
            acc[...] += p

    @pl.when(k == nk - 1)
    def _():
        epilogue([acc[...] + p for acc, p in zip(accs, parts)])


def _matmul(a, ws, mode, out_dtype, *, extra=(), tm=1024, tn=512, tk=None):
    m, kdim = a.shape
    n = ws[0].shape[1]
    tm = _pick(m, tm, BF16_ROWS)
    tn = _pick(n, tn, LANES)
    tk = kdim if tk is None else _pick(kdim, tk, LANES)
    nk = kdim // tk
    grid = (m // tm, n // tn, nk)
    in_specs = [pl.BlockSpec((tm, tk), lambda i, j, k: (i, k))]
    in_specs += [pl.BlockSpec((tk, tn), lambda i, j, k: (k, j)) for _ in ws]
    row = pl.BlockSpec((1, tn), lambda i, j, k: (0, j))
    head = pl.BlockSpec((1, HEAD_DIM), lambda i, j, k: (0, 0))
    ops = []
    if mode == "bias":
        in_specs += [row]
        ops = [extra[0].reshape(1, n)]
    elif mode == "resid":
        in_specs += [pl.BlockSpec((tm, tn), lambda i, j, k: (i, j)), row]
        ops = [extra[0], extra[1].reshape(1, n)]
    elif mode == "headnorm":
        in_specs += [head]
        ops = [extra[0].reshape(1, HEAD_DIM)]
    elif mode == "headrope":
        tab = pl.BlockSpec((tm, HEAD_DIM), lambda i, j, k: (i, 0))
        in_specs += [head, tab, tab]
        ops = [extra[0].reshape(1, HEAD_DIM), extra[1], extra[2]]
    n_acc = len(ws) if nk > 1 else 0
    return pl.pallas_call(
        functools.partial(_mm_kernel, mode=mode, nk=nk),
        grid=grid,
        in_specs=in_specs,
        out_specs=pl.BlockSpec((tm, tn), lambda i, j, k: (i, j)),
        out_shape=jax.ShapeDtypeStruct((m, n), out_dtype),
        scratch_shapes=[pltpu.VMEM((tm, tn), F32) for _ in range(n_acc)],
        compiler_params=_params("parallel", "parallel", "arbitrary"),
        name="mm_" + mode,
    )(a, *ws, *ops)


def _attn_kernel(*refs, kind, n_kv, lam_init):
    if kind == "gqa":
        q_ref, k_ref, v_ref, o_ref, m_sc, l_sc, acc_sc = refs
        n_q = GQA_GROUP
    else:
        q_ref, k_ref, v_ref, lq1, lk1, lq2, lk2, sg_ref, o_ref, m_sc, l_sc, acc_sc = refs
        n_q = 2
    kv = pl.program_id(2)
    scale = HEAD_DIM ** -0.5

    @pl.when(kv == 0)
    def _():
        m_sc[...] = jnp.full(m_sc.shape, -jnp.inf, F32)
        l_sc[...] = jnp.zeros(l_sc.shape, F32)
        acc_sc[...] = jnp.zeros(acc_sc.shape, F32)

    v = v_ref[...]
    for h in range(n_q):
        q = q_ref[:, h * HEAD_DIM:(h + 1) * HEAD_DIM]
        kh = 0 if kind == "gqa" else h
        k = k_ref[:, kh * HEAD_DIM:(kh + 1) * HEAD_DIM]
        s = lax.dot_general(q, k, (((1,), (1,)), ((), ())), preferred_element_type=F32) * scale
        m_prev = m_sc[h][:, 0:1]
        l_prev = l_sc[h][:, 0:1]
        m_new = jnp.maximum(m_prev, jnp.max(s, axis=-1, keepdims=True))
        p = jnp.exp(s - m_new)
        alpha = jnp.exp(m_prev - m_new)
        l_new = alpha * l_prev + jnp.sum(p, axis=-1, keepdims=True)
        acc_sc[h] = alpha * acc_sc[h] + jnp.dot(p.astype(v.dtype), v, preferred_element_type=F32)
        m_sc[h] = jnp.broadcast_to(m_new, m_sc.shape[1:])
        l_sc[h] = jnp.broadcast_to(l_new, l_sc.shape[1:])

    @pl.when(kv == n_kv - 1)
    def _():
        if kind == "gqa":
            for h in range(n_q):
                o = acc_sc[h] / l_sc[h][:, 0:1]
                o_ref[:, h * HEAD_DIM:(h + 1) * HEAD_DIM] = o.astype(o_ref.dtype)
        else:
            lam = (jnp.exp(jnp.sum(lq1[...] * lk1[...], axis=-1, keepdims=True))
                   - jnp.exp(jnp.sum(lq2[...] * lk2[...], axis=-1, keepdims=True)) + lam_init)
            o = acc_sc[0] / l_sc[0][:, 0:1] - lam * (acc_sc[1] / l_sc[1][:, 0:1])
            ms = jnp.mean(o * o, axis=-1, keepdims=True)
            o = o * lax.rsqrt(ms + NORM_EPS) * sg_ref[...] * (1.0 - lam_init)
            o_ref[...] = o.astype(o_ref.dtype)


def _attention(q, k, v, kind, *, lam_params=None, sub_gain=None, lam_init=0.0):
    lq, d = q.shape
    t = k.shape[0]
    if kind == "gqa":
        n_heads = d // (GQA_GROUP * HEAD_DIM)
        qw, kw, vw, n_q = GQA_GROUP * HEAD_DIM, HEAD_DIM, HEAD_DIM, GQA_GROUP
    else:
        n_heads = d // (2 * HEAD_DIM)
        qw, kw, vw, n_q = 2 * HEAD_DIM, 2 * HEAD_DIM, 2 * HEAD_DIM, 2
    tq = _pick(lq, 512, BF16_ROWS)
    tk = _pick(t, 1280, LANES)
    n_kv = t // tk
    in_specs = [
        pl.BlockSpec((tq, qw), lambda h, i, j: (i, h)),
        pl.BlockSpec((tk, kw), lambda h, i, j: (j, h)),
        pl.BlockSpec((tk, vw), lambda h, i, j: (j, h)),
    ]
    ops = [q, k, v]
    if kind == "diff":
        vec = pl.BlockSpec((1, HEAD_DIM), lambda h, i, j: (0, 0))
        in_specs += [vec, vec, vec, vec, pl.BlockSpec((1, 2 * HEAD_DIM), lambda h, i, j: (0, 0))]
        ops += [p.reshape(1, HEAD_DIM) for p in lam_params] + [sub_gain.reshape(1, 2 * HEAD_DIM)]
    return pl.pallas_call(
        functools.partial(_attn_kernel, kind=kind, n_kv=n_kv, lam_init=lam_init),
        grid=(n_heads, lq // tq, n_kv),
        in_specs=in_specs,
        out_specs=pl.BlockSpec((tq, qw), lambda h, i, j: (i, h)),
        out_shape=jax.ShapeDtypeStruct((lq, d), BF16),
        scratch_shapes=[
            pltpu.VMEM((n_q, tq, LANES), F32),
            pltpu.VMEM((n_q, tq, LANES), F32),
            pltpu.VMEM((n_q, tq, vw), F32),
        ],
        compiler_params=_params("parallel", "parallel", "arbitrary"),
        name="attn_" + kind,
    )(*ops)


def _sconv_kernel(z_ref, zp_ref, zn_ref, w_ref, b_ref, o_ref, *, n_row_blocks):
    i = pl.program_id(1)
    z = z_ref[...]
    tr = z.shape[0]
    prev_row = jnp.where(i == 0, 0.0, zp_ref[SUBLANES - 1:SUBLANES, :])
    next_row = jnp.where(i == n_row_blocks - 1, 0.0, zn_ref[0:1, :])
    row = lax.broadcasted_iota(jnp.int32, z.shape, 0)
    z_before = jnp.where(row == 0, prev_row, pltpu.roll(z, 1, axis=0))
    z_after = jnp.where(row == tr - 1, next_row, pltpu.roll(z, tr - 1, axis=0))
    w = w_ref[...]
    o_ref[0] = b_ref[...] + z_before * w[0:1] + z * w[1:2] + z_after * w[2:3]


def _short_conv(z, conv_w, conv_b, n_split):
    l, c = z.shape
    d = c // n_split
    tr = _pick(l, 512, SUBLANES)
    tc = _pick(d, 1024, LANES)
    ncb = d // tc
    nrb = l // tr
    r8 = tr // SUBLANES
    last8 = l // SUBLANES - 1
    return pl.pallas_call(
        functools.partial(_sconv_kernel, n_row_blocks=nrb),
        grid=(n_split, nrb, ncb),
        in_specs=[
            pl.BlockSpec((tr, tc), lambda s, i, j: (i, s * ncb + j)),
            pl.BlockSpec((SUBLANES, tc), lambda s, i, j: (jnp.maximum(i * r8 - 1, 0), s * ncb + j)),
            pl.BlockSpec((SUBLANES, tc), lambda s, i, j: (jnp.minimum((i + 1) * r8, last8), s * ncb + j)),
            pl.BlockSpec((conv_w.shape[0], tc), lambda s, i, j: (0, s * ncb + j)),
            pl.BlockSpec((1, tc), lambda s, i, j: (0, s * ncb + j)),
        ],
        out_specs=pl.BlockSpec((1, tr, tc), lambda s, i, j: (s, i, j)),
        out_shape=jax.ShapeDtypeStruct((n_split, l, d), F32),
        compiler_params=_params("parallel", "parallel", "parallel"),
        name="short_conv",
    )(z, z, z, conv_w, conv_b.reshape(1, c))


def _filter_mlp_kernel(f_ref, w1_ref, b1_ref, w2_ref, b2_ref, fr_ref, o_ref):
    g = jnp.dot(f_ref[...], w1_ref[...], precision=HIGHEST, preferred_element_type=F32)
    g = jnp.sin(fr_ref[0:1] * (g + b1_ref[...]))
    g = jnp.dot(g, w2_ref[...], precision=HIGHEST, preferred_element_type=F32)
    o_ref[...] = jnp.sin(fr_ref[1:2] * (g + b2_ref[...]))


def _filter_taps_kernel(g_ref, w3_ref, dl_ref, o_ref, asum_ref, *, n_tok):
    i = pl.program_id(1)
    tr = g_ref.shape[0]
    n = i * tr + lax.broadcasted_iota(jnp.int32, (tr, 1), 0)
    tau = jnp.where(n < n_tok, n, 2 * n_tok - n)
    t = tau.astype(F32) * (1.0 / (n_tok - 1))
    taps = jnp.dot(g_ref[...], w3_ref[...], precision=HIGHEST, preferred_element_type=F32)
    taps = taps * jnp.exp(-t * dl_ref[...])
    taps = jnp.where(n == n_tok, 0.0, taps)
    o_ref[...] = taps

    @pl.when(i == 0)
    def _():
        asum_ref[...] = jnp.zeros(asum_ref.shape, F32)

    asum_ref[...] += jnp.sum(jnp.abs(taps), axis=0, keepdims=True)


def _hyena_filter_taps(n_tok, d, f_w1, f_b1, f_w2, f_b2, f_w3, f_freq):
    width = f_w1.shape[1]
    t = jnp.linspace(0.0, 1.0, n_tok, dtype=F32)[:, None]
    bands = (HY_EMB_DIM - 1) // 2
    w = (2.0 * math.pi / n_tok) * jnp.arange(n_tok, dtype=F32)[:, None]
    f = jnp.linspace(1e-4, bands - 1, bands, dtype=F32)[None, :]
    feats = jnp.concatenate([t, jnp.cos(f * w), -jnp.sin(f * w)], axis=-1)
    feats2 = jnp.concatenate([feats, feats[:1], feats[:0:-1]], axis=0)
    feats2 = jnp.pad(feats2, ((0, 0), (0, LANES - HY_EMB_DIM)))
    pw = LANES - width
    w1 = jnp.pad(f_w1, ((0, LANES - HY_EMB_DIM), (0, pw)))
    w2 = jnp.pad(f_w2, ((0, pw), (0, pw)))
    b1 = jnp.pad(f_b1, (0, pw)).reshape(1, LANES)
    b2 = jnp.pad(f_b2, (0, pw)).reshape(1, LANES)
    freq = jnp.pad(f_freq, ((0, 0), (0, pw)))
    n2l = 2 * n_tok
    tr = _pick(n_tok, 512, SUBLANES)
    full = lambda shape: pl.BlockSpec(shape, lambda i: (0,) * len(shape))
    g = pl.pallas_call(
        _filter_mlp_kernel,
        grid=(n2l // tr,),
        in_specs=[pl.BlockSpec((tr, LANES), lambda i: (i, 0)), full((LANES, LANES)), full((1, LANES)),
                  full((LANES, LANES)), full((1, LANES)), full((2, LANES))],
        out_specs=pl.BlockSpec((tr, LANES), lambda i: (i, 0)),
        out_shape=jax.ShapeDtypeStruct((n2l, LANES), F32),
        compiler_params=_params("parallel"),
        name="hyena_filter_mlp",
    )(feats2, w1, b1, w2, b2, freq)

    c = HY_ORDER * d
    w3 = jnp.pad(f_w3, ((0, pw), (0, 0)))
    max_decay = math.log(HY_DECAY_TARGET) / HY_FAST_DECAY_PCT
    min_decay = math.log(HY_DECAY_TARGET) / HY_SLOW_DECAY_PCT
    deltas = jnp.abs(jnp.linspace(min_decay, max_decay, d, dtype=F32))
    deltas = jnp.tile(deltas, HY_ORDER).reshape(1, c)
    tc = _pick(c, 1024, LANES)
    ncb = c // tc
    half = n_tok // tr
    taps, asum = pl.pallas_call(
        functools.partial(_filter_taps_kernel, n_tok=n_tok),
        grid=(ncb, n2l // tr),
        in_specs=[
            pl.BlockSpec((tr, LANES), lambda j, i: (i, 0)),
            pl.BlockSpec((LANES, tc), lambda j, i: (0, j + jnp.where(i >= half, ncb, 0))),
            pl.BlockSpec((1, tc), lambda j, i: (0, j)),
        ],
        out_specs=[pl.BlockSpec((tr, tc), lambda j, i: (i, j)), pl.BlockSpec((1, tc), lambda j, i: (0, j))],
        out_shape=[jax.ShapeDtypeStruct((n2l, c), F32), jax.ShapeDtypeStruct((1, c), F32)],
        compiler_params=_params("parallel", "arbitrary"),
        name="hyena_filter_taps",
    )(g, w3, deltas)
    return taps, asum


def _split3(x):
    hi = x.astype(BF16)
    lo = (x - hi.astype(F32)).astype(BF16)
    return jnp.concatenate([hi, lo, hi], axis=0)


def _mat3(m64):
    m = jnp.asarray(m64, F32)
    hi = m.astype(BF16)
    lo = (m - hi.astype(F32)).astype(BF16)
    return jnp.concatenate([hi, hi, lo], axis=1)


def _dft_plan(n_tok):
    n = 2 * n_tok
    n1 = 1 << ((n.bit_length() - 1 + 1) // 2)
    n2 = n // n1
    assert n1 * n2 == n and n2 >= SUBLANES, (n_tok, n1, n2)
    k1 = n1 // 2 + 1
    k1p = -(-k1 // BF16_ROWS) * BF16_ROWS
    return n, n1, n2, k1, k1p


def _dft_constants(n_tok):
    n, n1, n2, k1, k1p = _dft_plan(n_tok)
    two_pi = 2.0 * np.pi
    kk = np.arange(k1p)[:, None]
    valid = (kk < k1).astype(np.float64)
    ang = two_pi * ((kk * np.arange(n1)[None, :]) % n1) / n1
    f1 = np.concatenate([np.cos(ang) * valid, -np.sin(ang) * valid], axis=0)
    a2 = two_pi * ((np.arange(n2)[:, None] * np.arange(n2)[None, :]) % n2) / n2
    c2, s2 = np.cos(a2), np.sin(a2)
    f2 = np.block([[c2, s2], [-s2, c2]])
    g2 = np.block([[c2, -s2], [s2, c2]])
    wk = np.where((np.arange(k1p) == 0) | (np.arange(k1p) == n1 // 2), 1.0, 2.0) * (np.arange(k1p) < k1)
    a3 = two_pi * ((np.arange(n1 // 2)[:, None] * np.arange(k1p)[None, :]) % n1) / n1
    g3 = np.concatenate([np.cos(a3) * wk[None, :], -np.sin(a3) * wk[None, :]], axis=1) / n
    at = two_pi * (kk * np.arange(n2)[None, :]) / n
    return dict(
        f1_full=_mat3(f1), f1_half=_mat3(f1[:, :n1 // 2]), f2=_mat3(f2), g2=_mat3(g2), g3=_mat3(g3),
        tw_cos=jnp.asarray(np.cos(at), F32), tw_sin=jnp.asarray(np.sin(at), F32),
        tw_cos_t=jnp.asarray(np.cos(at).T, F32), tw_sin_t=jnp.asarray(np.sin(at).T, F32),
    )


def _table_column(tab_ref, idx):
    lane = lax.broadcasted_iota(jnp.int32, tab_ref.shape, 1)
    return jnp.sum(jnp.where(lane == idx, tab_ref[...], 0.0), axis=1, keepdims=True)


def _fft1_kernel(x_ref, f_ref, tc_ref, ts_ref, are_ref, aim_ref, *, k1p, blocks_per_n2):
    n2 = pl.program_id(0) // blocks_per_n2
    a = jnp.dot(f_ref[...], _split3(x_ref[...]), preferred_element_type=F32)
    re, im = a[:k1p], a[k1p:]
    c = _table_column(tc_ref, n2)
    s = _table_column(ts_ref, n2)
    are_ref[...] = re * c + im * s
    aim_ref[...] = im * c - re * s


def _fft_stage1(x2d, f1, consts, n_tok, n_ch):
    _, n1, n2, _, k1p = _dft_plan(n_tok)
    n1c, cols = x2d.shape
    tcol = _pick(n_ch, 2048, LANES)
    tab = pl.BlockSpec((k1p, n2), lambda j: (0, 0))
    out = pl.BlockSpec((k1p, tcol), lambda j: (0, j))
    return pl.pallas_call(
        functools.partial(_fft1_kernel, k1p=k1p, blocks_per_n2=n_ch // tcol),
        grid=(cols // tcol,),
        in_specs=[pl.BlockSpec((n1c, tcol), lambda j: (0, j)),
                  pl.BlockSpec((2 * k1p, 3 * n1c), lambda j: (0, 0)), tab, tab],
        out_specs=[out, out],
        out_shape=[jax.ShapeDtypeStruct((k1p, cols), F32)] * 2,
        compiler_params=_params("parallel"),
        name="hyena_fft1",
    )(x2d, f1, consts["tw_cos"], consts["tw_sin"])


def _fft2_spec_kernel(are_ref, aim_ref, f_ref, asum_ref, sre_ref, sim_ref, *, tk1, n2):
    inv = 1.0 / (asum_ref[...] + NORM_EPS)

    def body(r, carry):
        a = jnp.concatenate([are_ref[r], aim_ref[r]], axis=0)
        x = jnp.dot(f_ref[...], _split3(a), preferred_element_type=F32)
        sre_ref[r] = x[:n2] * inv
        sim_ref[r] = x[n2:] * inv
        return carry

    lax.fori_loop(0, tk1, body, 0)


def _fft2_conv_kernel(are_ref, aim_ref, sre_ref, sim_ref, f_ref, g_ref, tc_ref, ts_ref,
                      bre_ref, bim_ref, *, tk1, n2):
    i = pl.program_id(0)

    def body(r, carry):
        a = jnp.concatenate([are_ref[r], aim_ref[r]], axis=0)
        x = jnp.dot(f_ref[...], _split3(a), preferred_element_type=F32)
        xr, xi = x[:n2], x[n2:]
        sr, si = sre_ref[r], sim_ref[r]
        y = jnp.concatenate([xr * sr - xi * si, xr * si + xi * sr], axis=0)
        b = jnp.dot(g_ref[...], _split3(y), preferred_element_type=F32)
        br, bi = b[:n2], b[n2:]
        c = _table_column(tc_ref, i * tk1 + r)
        s = _table_column(ts_ref, i * tk1 + r)
        bre_ref[r] = br * c - bi * s
        bim_ref[r] = bi * c + br * s
        return carry

    lax.fori_loop(0, tk1, body, 0)


def _fft_stage2(are, aim, consts, n_tok, n_ch, *, spec=None, asum=None, spec_offset=0):
    _, n1, n2, _, k1p = _dft_plan(n_tok)
    tk1 = SUBLANES
    td = _pick(n_ch, 512, LANES)
    blk = pl.BlockSpec((tk1, n2, td), lambda i, j: (i, 0, j))
    mat = pl.BlockSpec((2 * n2, 6 * n2), lambda i, j: (0, 0))
    grid = (k1p // tk1, n_ch // td)
    out_shape = [jax.ShapeDtypeStruct((k1p, n2, n_ch), F32)] * 2
    if spec is None:
        return pl.pallas_call(
            functools.partial(_fft2_spec_kernel, tk1=tk1, n2=n2),
            grid=grid,
            in_specs=[blk, blk, mat, pl.BlockSpec((1, td), lambda i, j: (0, j))],
            out_specs=[blk, blk],
            out_shape=out_shape,
            compiler_params=_params("parallel", "parallel"),
            name="hyena_fft2_spec",
        )(are, aim, consts["f2"], asum)
    off = spec_offset // td
    sblk = pl.BlockSpec((tk1, n2, td), lambda i, j: (i, 0, j + off))
    tab = pl.BlockSpec((n2, k1p), lambda i, j: (0, 0))
    return pl.pallas_call(
        functools.partial(_fft2_conv_kernel, tk1=tk1, n2=n2),
        grid=grid,
        in_specs=[blk, blk, sblk, sblk, mat, mat, tab, tab],
        out_specs=[blk, blk],
        out_shape=out_shape,
        compiler_params=_params("parallel", "parallel"),
        name="hyena_fft2_conv",
    )(are, aim, spec[0], spec[1], consts["f2"], consts["g2"], consts["tw_cos_t"], consts["tw_sin_t"])


def _fft3_kernel(bre_ref, bim_ref, g_ref, y_ref, gate_ref, bias_ref, o_ref):
    b = jnp.concatenate([bre_ref[...], bim_ref[...]], axis=0)
    conv = jnp.dot(g_ref[...], _split3(b), preferred_element_type=F32)
    o_ref[...] = (gate_ref[...] * (conv + y_ref[...] * bias_ref[...])).astype(o_ref.dtype)


def _fft_stage3(bre, bim, consts, n_tok, n_ch, y2d, gate2d, bias, out_dtype):
    _, n1, n2, _, k1p = _dft_plan(n_tok)
    cols = n2 * n_ch
    tcol = _pick(n_ch, 2048, LANES)
    per = n_ch // tcol
    inb = pl.BlockSpec((k1p, tcol), lambda j: (0, j))
    sig = pl.BlockSpec((n1 // 2, tcol), lambda j: (0, j))
    return pl.pallas_call(
        _fft3_kernel,
        grid=(cols // tcol,),
        in_specs=[inb, inb, pl.BlockSpec((n1 // 2, 6 * k1p), lambda j: (0, 0)), sig, sig,
                  pl.BlockSpec((1, tcol), lambda j: (0, j % per))],
        out_specs=sig,
        out_shape=jax.ShapeDtypeStruct((n1 // 2, cols), out_dtype),
        compiler_params=_params("parallel"),
        name="hyena_fft3",
    )(bre, bim, consts["g3"], y2d, gate2d, bias.reshape(1, n_ch))


def _hyena_mixer_core(h, w_in, b_in, conv_w, conv_b, f_w1, f_b1, f_w2, f_b2, f_w3, f_freq, f_bias):
    n_tok, d = h.shape
    _, n1, n2, _, k1p = _dft_plan(n_tok)
    consts = _dft_constants(n_tok)
    z = _matmul(h, (w_in,), "bias", F32, extra=(b_in,))
    zs = _short_conv(z, conv_w, conv_b, HY_ORDER + 1)

    c = HY_ORDER * d
    taps, asum = _hyena_filter_taps(n_tok, d, f_w1, f_b1, f_w2, f_b2, f_w3, f_freq)
    tre, tim = _fft_stage1(taps.reshape(n1, n2 * c), consts["f1_full"], consts, n_tok, c)
    spec = _fft_stage2(tre.reshape(k1p, n2, c), tim.reshape(k1p, n2, c), consts, n_tok, c, asum=asum)

    y = zs[0].reshape(n1 // 2, n2 * d)
    for n in range(HY_ORDER):
        are, aim = _fft_stage1(y, consts["f1_half"], consts, n_tok, d)
        bre, bim = _fft_stage2(are.reshape(k1p, n2, d), aim.reshape(k1p, n2, d), consts, n_tok, d,
                               spec=spec, spec_offset=n * d)
        last = n == HY_ORDER - 1
        y = _fft_stage3(bre.reshape(k1p, n2 * d), bim.reshape(k1p, n2 * d), consts, n_tok, d, y,
                        zs[n + 1].reshape(n1 // 2, n2 * d), f_bias[n], BF16 if last else F32)
    return y.reshape(n_tok, d)


def _axial_rope_tables(n_tok):
    rows = n_tok // GRID_W
    row = jnp.repeat(jnp.arange(rows, dtype=F32), GRID_W)
    col = (jnp.arange(rows * GRID_W) % GRID_W).astype(F32)
    n_pair = HEAD_DIM // 4
    inv_freq = ROPE_THETA ** (-jnp.arange(n_pair, dtype=F32) / n_pair)
    ang = jnp.concatenate([row[:, None] * inv_freq, col[:, None] * inv_freq], axis=-1)
    cos, sin = jnp.cos(ang), jnp.sin(ang)
    return jnp.concatenate([cos, cos], axis=-1), jnp.concatenate([-sin, sin], axis=-1)


def _split_half_cols(w):
    k, n = w.shape
    return w.reshape(k, n // HEAD_DIM, HEAD_DIM // 2, 2).transpose(0, 1, 3, 2).reshape(k, n)


def _split_half_vec(g):
    return g.reshape(HEAD_DIM // 2, 2).T.reshape(HEAD_DIM)


def kernel(x, c, ctx, c_ctx, ada_a, ada_b, ada_bias, norm1_g, norm2_g, ffn_w_gate, ffn_w_up, ffn_w_down,
           hy_w_in, hy_b_in, hy_conv_w, hy_conv_b, hy_f_w1, hy_f_b1, hy_f_w2, hy_f_b2, hy_f_w3, hy_f_freq,
           hy_f_bias, hy_w_out, gqa_w_q, gqa_w_k, gqa_w_v, gqa_q_norm, gqa_k_norm, gqa_w_o,
           diff_w_q, diff_w_k, diff_w_v, diff_q_norm, diff_k_norm, diff_lambda_q1, diff_lambda_k1,
           diff_lambda_q2, diff_lambda_k2, diff_sub_norm, diff_w_o):
    assert x.shape[0] == 1 and ctx.shape[0] == 1
    depth = ada_a.shape[0]
    n_tok, d = x.shape[1], x.shape[2]
    xs = x[0]
    cs = ctx[0]
    cos, sin = _axial_rope_tables(n_tok)

    cc = jnp.concatenate([c.reshape(1, d), c_ctx.reshape(1, d), jnp.zeros((SUBLANES - 2, d), F32)], axis=0)
    mods = _ada_mods(cc, ada_a, ada_b, ada_bias)

    kinds = [i % N_MIXERS for i in range(depth)]
    for i in range(depth):
        kind = kinds[i]
        j = i // N_MIXERS
        reads_ctx = kind != MIX_HYENA
        update_ctx = any(kk != MIX_HYENA for kk in kinds[i + 1:])
        sh1, sc1, g1, sh2, sc2, g2 = [mods[i, 0, m * d:(m + 1) * d] for m in range(N_MOD)]
        csh1, csc1, cg1, csh2, csc2, cg2 = [mods[i, 1, m * d:(m + 1) * d] for m in range(N_MOD)]
        h = _norm_mod(xs, norm1_g[i], sh1, sc1)
        hc = _norm_mod(cs, norm1_g[i], csh1, csc1) if (reads_ctx or update_ctx) else None
        yc = None
        if kind == MIX_HYENA:
            w_in = hy_w_in[j].astype(BF16)
            w_out = hy_w_out[j].astype(BF16)
            args = (w_in, hy_b_in[j], hy_conv_w[j], hy_conv_b[j], hy_f_w1[j], hy_f_b1[j], hy_f_w2[j],
                    hy_f_b2[j], hy_f_w3[j], hy_f_freq[j], hy_f_bias[j])
            y = _hyena_mixer_core(h, *args)
            xs = _matmul(y, (w_out,), "resid", F32, extra=(xs, g1))
            if update_ctx:
                yc = _hyena_mixer_core(hc, *args)
                cs = _matmul(yc, (w_out,), "resid", F32, extra=(cs, cg1))
        elif kind == MIX_GQA:
            w_q = _split_half_cols(gqa_w_q[j]).astype(BF16)
            w_k = _split_half_cols(gqa_w_k[j]).astype(BF16)
            w_v = gqa_w_v[j].astype(BF16)
            w_o = gqa_w_o[j].astype(BF16)
            qg = _split_half_vec(gqa_q_norm[j])
            kg = _split_half_vec(gqa_k_norm[j])
            q = _matmul(h, (w_q,), "headrope", BF16, extra=(qg, cos, sin))
            k = _matmul(h, (w_k,), "headrope", BF16, extra=(kg, cos, sin))
            v = _matmul(h, (w_v,), "plain", BF16)
            kc = _matmul(hc, (w_k,), "headnorm", BF16, extra=(kg,))
            vc = _matmul(hc, (w_v,), "plain", BF16)
            o = _attention(q, jnp.concatenate([k, kc], axis=0), jnp.concatenate([v, vc], axis=0), "gqa")
            xs = _matmul(o, (w_o,), "resid", F32, extra=(xs, g1))
            if update_ctx:
                qc = _matmul(hc, (w_q,), "headnorm", BF16, extra=(qg,))
                oc = _attention(qc, kc, vc, "gqa")
                cs = _matmul(oc, (w_o,), "resid", F32, extra=(cs, cg1))
        else:
            w_q = _split_half_cols(diff_w_q[j]).astype(BF16)
            w_k = _split_half_cols(diff_w_k[j]).astype(BF16)
            w_v = diff_w_v[j].astype(BF16)
            w_o = diff_w_o[j].astype(BF16)
            qg = _split_half_vec(diff_q_norm[j])
            kg = _split_half_vec(diff_k_norm[j])
            lam_init = 0.8 - 0.6 * math.exp(-0.3 * i)
            lam_params = (diff_lambda_q1[j], diff_lambda_k1[j], diff_lambda_q2[j], diff_lambda_k2[j])
            attend = functools.partial(_attention, kind="diff", lam_params=lam_params,
                                       sub_gain=diff_sub_norm[j], lam_init=lam_init)
            q = _matmul(h, (w_q,), "headrope", BF16, extra=(qg, cos, sin))
            k = _matmul(h, (w_k,), "headrope", BF16, extra=(kg, cos, sin))
            v = _matmul(h, (w_v,), "plain", BF16)
            kc = _matmul(hc, (w_k,), "headnorm", BF16, extra=(kg,))
            vc = _matmul(hc, (w_v,), "plain", BF16)
            o = attend(q, jnp.concatenate([k, kc], axis=0), jnp.concatenate([v, vc], axis=0))
            xs = _matmul(o, (w_o,), "resid", F32, extra=(xs, g1))
            if update_ctx:
                qc = _matmul(hc, (w_q,), "headnorm", BF16, extra=(qg,))
                oc = attend(qc, kc, vc)
                cs = _matmul(oc, (w_o,), "resid", F32, extra=(cs, cg1))

        w_gate = ffn_w_gate[i].astype(BF16)
        w_up = ffn_w_up[i].astype(BF16)
        w_down = ffn_w_down[i].astype(BF16)
        h2 = _norm_mod(xs, norm2_g[i], sh2, sc2)
        act = _matmul(h2, (w_gate, w_up), "swiglu", BF16, tn=256)
        xs = _matmul(act, (w_down,), "resid", F32, extra=(xs, g2), tm=512, tn=256)
        if update_ctx:
            hc2 = _norm_mod(cs, norm2_g[i], csh2, csc2)
            actc = _matmul(hc2, (w_gate, w_up), "swiglu", BF16, tn=256)
            cs = _matmul(actc, (w_down,), "resid", F32, extra=(cs, cg2), tm=512, tn=256)
    return xs[None]
```

```python
import functools
import math

import numpy as np
import jax
import jax.numpy as jnp
from jax import lax
from jax.experimental import pallas as pl
from jax.experimental.pallas import tpu as pltpu

HEAD_DIM = 128
GRID_W = 64
GQA_GROUP = 4
N_MOD = 6
N_MIXERS = 3
MIX_HYENA, MIX_GQA, MIX_DIFF = 0, 1, 2
NORM_EPS = 1e-6
ROPE_THETA = 10000.0
HY_ORDER = 2
HY_EMB_DIM = 33
HY_DECAY_TARGET = 1e-2
HY_FAST_DECAY_PCT = 0.3
HY_SLOW_DECAY_PCT = 1.5
ATTN_Q_SCALE = HEAD_DIM ** -0.5 * math.log2(math.e)
V7X_MXU_WIDTH = 256
ATTN_COL_TILE = V7X_MXU_WIDTH

V7X_VMEM_BYTES = 64 * 1024 * 1024
VMEM_LIMIT_BYTES = V7X_VMEM_BYTES - 8 * 1024 * 1024
LANES = 128
SUBLANES = 8
BF16_ROWS = 16

F32 = jnp.float32
BF16 = jnp.bfloat16
HIGHEST = lax.Precision.HIGHEST


def _pick(dim, pref, align):
    if dim <= pref:
        return dim
    best = None
    for d in range(align, pref + 1, align):
        if dim % d == 0:
            best = d
    assert best is not None, (dim, pref, align)
    return best


def _params(*sem):
    return pltpu.CompilerParams(dimension_semantics=sem, vmem_limit_bytes=VMEM_LIMIT_BYTES)


def _silu(x):
    return x * (1.0 / (1.0 + jnp.exp(-x)))


def _ada_kernel(cc_ref, a_ref, b_ref, bias_ref, o_ref):
    s = _silu(cc_ref[...])
    t = jnp.dot(s, a_ref[0], precision=HIGHEST, preferred_element_type=F32)
    o_ref[0] = jnp.dot(t, b_ref[0], precision=HIGHEST, preferred_element_type=F32) + bias_ref[0]


def _ada_mods(cc, ada_a, ada_b, ada_bias):
    depth, d, r = ada_a.shape
    n = ada_b.shape[2]
    tn = _pick(n, 2048, LANES)
    return pl.pallas_call(
        _ada_kernel,
        grid=(depth, n // tn),
        in_specs=[
            pl.BlockSpec((SUBLANES, d), lambda l, j: (0, 0)),
            pl.BlockSpec((1, d, r), lambda l, j: (l, 0, 0)),
            pl.BlockSpec((1, r, tn), lambda l, j: (l, 0, j)),
            pl.BlockSpec((1, 1, tn), lambda l, j: (l, 0, j)),
        ],
        out_specs=pl.BlockSpec((1, SUBLANES, tn), lambda l, j: (l, 0, j)),
        out_shape=jax.ShapeDtypeStruct((depth, SUBLANES, n), F32),
        compiler_params=_params("parallel", "parallel"),
        name="ada_mods",
    )(cc, ada_a, ada_b, ada_bias.reshape(depth, 1, n))


def _norm_mod_kernel(x_ref, g_ref, sh_ref, sc_ref, o_ref):
    x = x_ref[...]
    ms = jnp.mean(x * x, axis=-1, keepdims=True)
    y = x * lax.rsqrt(ms + NORM_EPS) * g_ref[...]
    o_ref[...] = (y * (1.0 + sc_ref[...]) + sh_ref[...]).astype(o_ref.dtype)


def _norm_mod(x, gain, shift, scale):
    m, d = x.shape
    tr = _pick(m, 256, SUBLANES)
    vec = pl.BlockSpec((1, d), lambda i: (0, 0))
    return pl.pallas_call(
        _norm_mod_kernel,
        grid=(m // tr,),
        in_specs=[pl.BlockSpec((tr, d), lambda i: (i, 0)), vec, vec, vec],
        out_specs=pl.BlockSpec((tr, d), lambda i: (i, 0)),
        out_shape=jax.ShapeDtypeStruct((m, d), BF16),
        compiler_params=_params("parallel"),
        name="norm_mod",
    )(x, gain.reshape(1, d), shift.reshape(1, d), scale.reshape(1, d))


def _head_norm_rope(y, gain, cos, sin, out_scale):
    outs = []
    for h in range(y.shape[1] // HEAD_DIM):
        t = y[:, h * HEAD_DIM:(h + 1) * HEAD_DIM]
        ms = jnp.mean(t * t, axis=-1, keepdims=True)
        t = t * lax.rsqrt(ms + NORM_EPS) * gain
        if cos is not None:
            t = t * cos + pltpu.roll(t, HEAD_DIM // 2, axis=1) * sin
        if out_scale != 1.0:
            t = t * out_scale
        outs.append(t)
    return outs[0] if len(outs) == 1 else jnp.concatenate(outs, axis=1)


def _mm_kernel(*refs, mode, nk, out_scale):
    n_w = 2 if mode == "swiglu" else 1
    a_ref, w_refs = refs[0], refs[1:1 + n_w]
    n_extra = {"bias": 1, "plain": 0, "swiglu": 0, "resid": 2, "headnorm": 1, "headrope": 3}[mode]
    extra = refs[1 + n_w:1 + n_w + n_extra]
    o_ref = refs[1 + n_w + n_extra]
    accs = refs[2 + n_w + n_extra:]
    k = pl.program_id(2)

    a = a_ref[...]
    parts = [jnp.dot(a, w[...], preferred_element_type=F32) for w in w_refs]

    def epilogue(ys):
        y = ys[0]
        if mode == "bias":
            y = y + extra[0][...]
        elif mode == "swiglu":
            y = _silu(y) * ys[1]
        elif mode == "resid":
            y = extra[0][...] + extra[1][...] * y
        elif mode == "headnorm":
            y = _head_norm_rope(y, extra[0][...], None, None, out_scale)
        elif mode == "headrope":
            y = _head_norm_rope(y, extra[0][...], extra[1][...], extra[2][...], out_scale)
        o_ref[...] = y.astype(o_ref.dtype)

    if nk == 1:
        epilogue(parts)
        return

    @pl.when(k == 0)
    def _():
        for acc, p in zip(accs, parts):
            acc[...] = p

    @pl.when(jnp.logical_and(k > 0, k < nk - 1))
    def _():
        for acc, p in zip(accs, parts):
            acc[...] += p

    @pl.when(k == nk - 1)
    def _():
        epilogue([acc[...] + p for acc, p in zip(accs, parts)])


def _matmul(a, ws, mode, out_dtype, *, extra=(), tm=1024, tn=512, tk=None, out_scale=1.0):
    m, kdim = a.shape
    n = ws[0].shape[1]
    tm = _pick(m, tm, BF16_ROWS)
    tn = _pick(n, tn, LANES)
    tk = kdim if tk is None else _pick(kdim, tk, LANES)
    nk = kdim // tk
    grid = (m // tm, n // tn, nk)
    in_specs = [pl.BlockSpec((tm, tk), lambda i, j, k: (i, k))]
    in_specs += [pl.BlockSpec((tk, tn), lambda i, j, k: (k, j)) for _ in ws]
    row = pl.BlockSpec((1, tn), lambda i, j, k: (0, j))
    head = pl.BlockSpec((1, HEAD_DIM), lambda i, j, k: (0, 0))
    ops = []
    if mode == "bias":
        in_specs += [row]
        ops = [extra[0].reshape(1, n)]
    elif mode == "resid":
        in_specs += [pl.BlockSpec((tm, tn), lambda i, j, k: (i, j)), row]
        ops = [extra[0], extra[1].reshape(1, n)]
    elif mode == "headnorm":
        in_specs += [head]
        ops = [extra[0].reshape(1, HEAD_DIM)]
    elif mode == "headrope":
        tab = pl.BlockSpec((tm, HEAD_DIM), lambda i, j, k: (i, 0))
        in_specs += [head, tab, tab]
        ops = [extra[0].reshape(1, HEAD_DIM), extra[1], extra[2]]
    n_acc = len(ws) if nk > 1 else 0
    return pl.pallas_call(
        functools.partial(_mm_kernel, mode=mode, nk=nk, out_scale=out_scale),
        grid=grid,
        in_specs=in_specs,
        out_specs=pl.BlockSpec((tm, tn), lambda i, j, k: (i, j)),
        out_shape=jax.ShapeDtypeStruct((m, n), out_dtype),
        scratch_shapes=[pltpu.VMEM((tm, tn), F32) for _ in range(n_acc)],
        compiler_params=_params("parallel", "parallel", "arbitrary"),
        name="mm_" + mode,
    )(a, *ws, *ops)


def _attn_kernel(*refs, kind, n_kv, lam_init):
    if kind == "gqa":
        q_ref, k_ref, vt_ref, o_ref, m_sc, acc_sc = refs
        n_q = GQA_GROUP
    else:
        q_ref, k_ref, vt_ref, lq1, lk1, lq2, lk2, sg_ref, o_ref, m_sc, acc_sc = refs
        n_q = 2
    kv = pl.program_id(2)
    dv = vt_ref.shape[0] - BF16_ROWS

    @pl.when(kv == 0)
    def _():
        m_sc[...] = jnp.full(m_sc.shape, -jnp.inf, F32)
        acc_sc[...] = jnp.zeros(acc_sc.shape, F32)

    vt = vt_ref[...]
    tq = q_ref.shape[0]
    tc = min(tq, ATTN_COL_TILE)
    units = [(h, c) for h in range(n_q) for c in range(tq // tc)]

    def scores_t(u):
        h, c = units[u]
        q = q_ref[c * tc:(c + 1) * tc, h * HEAD_DIM:(h + 1) * HEAD_DIM]
        kh = 0 if kind == "gqa" else h
        k = k_ref[:, kh * HEAD_DIM:(kh + 1) * HEAD_DIM]
        return lax.dot_general(k, q, (((1,), (1,)), ((), ())), preferred_element_type=F32)

    def softmax_t(u, st):
        h, c = units[u]
        m_prev = m_sc[h, :, c * tc:(c + 1) * tc]
        m_new = jnp.maximum(m_prev, jnp.max(st, axis=0, keepdims=True))
        m_sc[h, :, c * tc:(c + 1) * tc] = m_new
        return jnp.exp2(st - m_new).astype(vt.dtype), jnp.exp2(m_prev - m_new)

    def accumulate(u, p, alpha):
        h, c = units[u]
        acc = acc_sc[h, :, c * tc:(c + 1) * tc]
        acc_sc[h, :, c * tc:(c + 1) * tc] = alpha * acc + jnp.dot(vt, p, preferred_element_type=F32)

    st, pa = {}, {}
    for i in range(len(units) + 2):
        if i < len(units):
            st[i] = scores_t(i)
        if 0 <= i - 1 < len(units):
            pa[i - 1] = softmax_t(i - 1, st.pop(i - 1))
        if 0 <= i - 2 < len(units):
            accumulate(i - 2, *pa.pop(i - 2))

    @pl.when(kv == n_kv - 1)
    def _():
        def normalised(h):
            acc = acc_sc[h]
            return acc[:dv] / acc[dv:dv + 1]

        if kind == "gqa":
            for h in range(n_q):
                o_ref[:, h * HEAD_DIM:(h + 1) * HEAD_DIM] = normalised(h).T.astype(o_ref.dtype)
        else:
            lam = (jnp.exp(jnp.sum(lq1[...] * lk1[...], axis=-1, keepdims=True))
                   - jnp.exp(jnp.sum(lq2[...] * lk2[...], axis=-1, keepdims=True)) + lam_init)
            o = (normalised(0) - lam * normalised(1)).T
            ms = jnp.mean(o * o, axis=-1, keepdims=True)
            o = o * lax.rsqrt(ms + NORM_EPS) * sg_ref[...] * (1.0 - lam_init)
            o_ref[...] = o.astype(o_ref.dtype)


def _attention(q, k, v, kind, *, lam_params=None, sub_gain=None, lam_init=0.0):
    lq, d = q.shape
    t = k.shape[0]
    if kind == "gqa":
        n_heads = d // (GQA_GROUP * HEAD_DIM)
        qw, kw, vw, n_q = GQA_GROUP * HEAD_DIM, HEAD_DIM, HEAD_DIM, GQA_GROUP
    else:
        n_heads = d // (2 * HEAD_DIM)
        qw, kw, vw, n_q = 2 * HEAD_DIM, 2 * HEAD_DIM, 2 * HEAD_DIM, 2
    tq = _pick(lq, 1024, ATTN_COL_TILE)
    tk = _pick(t, 1280, LANES)
    n_kv = t // tk
    vt = v.T.reshape(v.shape[1] // vw, vw, t)
    ones = jnp.zeros((vt.shape[0], BF16_ROWS, t), v.dtype).at[:, 0, :].set(1.0)
    vt = jnp.concatenate([vt, ones], axis=1)
    vr = vw + BF16_ROWS
    in_specs = [
        pl.BlockSpec((tq, qw), lambda h, i, j: (i, h)),
        pl.BlockSpec((tk, kw), lambda h, i, j: (j, h)),
        pl.BlockSpec((None, vr, tk), lambda h, i, j: (h, 0, j)),
    ]
    ops = [q, k, vt]
    if kind == "diff":
        vec = pl.BlockSpec((1, HEAD_DIM), lambda h, i, j: (0, 0))
        in_specs += [vec, vec, vec, vec, pl.BlockSpec((1, 2 * HEAD_DIM), lambda h, i, j: (0, 0))]
        ops += [p.reshape(1, HEAD_DIM) for p in lam_params] + [sub_gain.reshape(1, 2 * HEAD_DIM)]
    return pl.pallas_call(
        functools.partial(_attn_kernel, kind=kind, n_kv=n_kv, lam_init=lam_init),
        grid=(n_heads, lq // tq, n_kv),
        in_specs=in_specs,
        out_specs=pl.BlockSpec((tq, qw), lambda h, i, j: (i, h)),
        out_shape=jax.ShapeDtypeStruct((lq, d), BF16),
        scratch_shapes=[
            pltpu.VMEM((n_q, 1, tq), F32),
            pltpu.VMEM((n_q, vr, tq), F32),
        ],
        compiler_params=_params("parallel", "parallel", "arbitrary"),
        name="attn_" + kind,
    )(*ops)


def _sconv_kernel(z_ref, zp_ref, zn_ref, w_ref, b_ref, o_ref, *, n_row_blocks):
    i = pl.program_id(1)
    z = z_ref[...]
    tr = z.shape[0]
    prev_row = jnp.where(i == 0, 0.0, zp_ref[SUBLANES - 1:SUBLANES, :])
    next_row = jnp.where(i == n_row_blocks - 1, 0.0, zn_ref[0:1, :])
    row = lax.broadcasted_iota(jnp.int32, z.shape, 0)
    z_before = jnp.where(row == 0, prev_row, pltpu.roll(z, 1, axis=0))
    z_after = jnp.where(row == tr - 1, next_row, pltpu.roll(z, tr - 1, axis=0))
    w = w_ref[...]
    o_ref[0] = b_ref[...] + z_before * w[0:1] + z * w[1:2] + z_after * w[2:3]


def _short_conv(z, conv_w, conv_b, n_split):
    l, c = z.shape
    d = c // n_split
    tr = _pick(l, 512, SUBLANES)
    tc = _pick(d, 1024, LANES)
    ncb = d // tc
    nrb = l // tr
    r8 = tr // SUBLANES
    last8 = l // SUBLANES - 1
    return pl.pallas_call(
        functools.partial(_sconv_kernel, n_row_blocks=nrb),
        grid=(n_split, nrb, ncb),
        in_specs=[
            pl.BlockSpec((tr, tc), lambda s, i, j: (i, s * ncb + j)),
            pl.BlockSpec((SUBLANES, tc), lambda s, i, j: (jnp.maximum(i * r8 - 1, 0), s * ncb + j)),
            pl.BlockSpec((SUBLANES, tc), lambda s, i, j: (jnp.minimum((i + 1) * r8, last8), s * ncb + j)),
            pl.BlockSpec((conv_w.shape[0], tc), lambda s, i, j: (0, s * ncb + j)),
            pl.BlockSpec((1, tc), lambda s, i, j: (0, s * ncb + j)),
        ],
        out_specs=pl.BlockSpec((1, tr, tc), lambda s, i, j: (s, i, j)),
        out_shape=jax.ShapeDtypeStruct((n_split, l, d), F32),
        compiler_params=_params("parallel", "parallel", "parallel"),
        name="short_conv",
    )(z, z, z, conv_w, conv_b.reshape(1, c))


def _filter_mlp_kernel(f_ref, w1_ref, b1_ref, w2_ref, b2_ref, fr_ref, o_ref):
    g = jnp.dot(f_ref[...], w1_ref[...], precision=HIGHEST, preferred_element_type=F32)
    g = jnp.sin(fr_ref[0:1] * (g + b1_ref[...]))
    g = jnp.dot(g, w2_ref[...], precision=HIGHEST, preferred_element_type=F32)
    o_ref[...] = jnp.sin(fr_ref[1:2] * (g + b2_ref[...]))


def _filter_taps_kernel(g_ref, w3_ref, dl_ref, o_ref, asum_ref, *, n_tok):
    i = pl.program_id(1)
    tr = g_ref.shape[0]
    n = i * tr + lax.broadcasted_iota(jnp.int32, (tr, 1), 0)
    tau = jnp.where(n < n_tok, n, 2 * n_tok - n)
    t = tau.astype(F32) * (1.0 / (n_tok - 1))
    taps = jnp.dot(g_ref[...], w3_ref[...], precision=HIGHEST, preferred_element_type=F32)
    taps = taps * jnp.exp(-t * dl_ref[...])
    taps = jnp.where(n == n_tok, 0.0, taps)
    o_ref[...] = taps

    @pl.when(i == 0)
    def _():
        asum_ref[...] = jnp.zeros(asum_ref.shape, F32)

    asum_ref[...] += jnp.sum(jnp.abs(taps), axis=0, keepdims=True)


def _hyena_filter_taps(n_tok, d, f_w1, f_b1, f_w2, f_b2, f_w3, f_freq):
    width = f_w1.shape[1]
    t = jnp.linspace(0.0, 1.0, n_tok, dtype=F32)[:, None]
    bands = (HY_EMB_DIM - 1) // 2
    w = (2.0 * math.pi / n_tok) * jnp.arange(n_tok, dtype=F32)[:, None]
    f = jnp.linspace(1e-4, bands - 1, bands, dtype=F32)[None, :]
    feats = jnp.concatenate([t, jnp.cos(f * w), -jnp.sin(f * w)], axis=-1)
    feats2 = jnp.concatenate([feats, feats[:1], feats[:0:-1]], axis=0)
    feats2 = jnp.pad(feats2, ((0, 0), (0, LANES - HY_EMB_DIM)))
    pw = LANES - width
    w1 = jnp.pad(f_w1, ((0, LANES - HY_EMB_DIM), (0, pw)))
    w2 = jnp.pad(f_w2, ((0, pw), (0, pw)))
    b1 = jnp.pad(f_b1, (0, pw)).reshape(1, LANES)
    b2 = jnp.pad(f_b2, (0, pw)).reshape(1, LANES)
    freq = jnp.pad(f_freq, ((0, 0), (0, pw)))
    n2l = 2 * n_tok
    tr = _pick(n_tok, 512, SUBLANES)
    full = lambda shape: pl.BlockSpec(shape, lambda i: (0,) * len(shape))
    g = pl.pallas_call(
        _filter_mlp_kernel,
        grid=(n2l // tr,),
        in_specs=[pl.BlockSpec((tr, LANES), lambda i: (i, 0)), full((LANES, LANES)), full((1, LANES)),
                  full((LANES, LANES)), full((1, LANES)), full((2, LANES))],
        out_specs=pl.BlockSpec((tr, LANES), lambda i: (i, 0)),
        out_shape=jax.ShapeDtypeStruct((n2l, LANES), F32),
        compiler_params=_params("parallel"),
        name="hyena_filter_mlp",
    )(feats2, w1, b1, w2, b2, freq)

    c = HY_ORDER * d
    w3 = jnp.pad(f_w3, ((0, pw), (0, 0)))
    max_decay = math.log(HY_DECAY_TARGET) / HY_FAST_DECAY_PCT
    min_decay = math.log(HY_DECAY_TARGET) / HY_SLOW_DECAY_PCT
    deltas = jnp.abs(jnp.linspace(min_decay, max_decay, d, dtype=F32))
    deltas = jnp.tile(deltas, HY_ORDER).reshape(1, c)
    tc = _pick(c, 1024, LANES)
    ncb = c // tc
    half = n_tok // tr
    taps, asum = pl.pallas_call(
        functools.partial(_filter_taps_kernel, n_tok=n_tok),
        grid=(ncb, n2l // tr),
        in_specs=[
            pl.BlockSpec((tr, LANES), lambda j, i: (i, 0)),
            pl.BlockSpec((LANES, tc), lambda j, i: (0, j + jnp.where(i >= half, ncb, 0))),
            pl.BlockSpec((1, tc), lambda j, i: (0, j)),
        ],
        out_specs=[pl.BlockSpec((tr, tc), lambda j, i: (i, j)), pl.BlockSpec((1, tc), lambda j, i: (0, j))],
        out_shape=[jax.ShapeDtypeStruct((n2l, c), F32), jax.ShapeDtypeStruct((1, c), F32)],
        compiler_params=_params("parallel", "arbitrary"),
        name="hyena_filter_taps",
    )(g, w3, deltas)
    return taps, asum


def _split3(x):
    hi = x.astype(BF16)
    lo = (x - hi.astype(F32)).astype(BF16)
    return jnp.concatenate([hi, lo, hi], axis=0)


def _mat3(m64):
    m = jnp.asarray(m64, F32)
    hi = m.astype(BF16)
    lo = (m - hi.astype(F32)).astype(BF16)
    return jnp.concatenate([hi, hi, lo], axis=1)


def _dft_plan(n_tok):
    n = 2 * n_tok
    n1 = 1 << ((n.bit_length() - 1 + 1) // 2)
    n2 = n // n1
    assert n1 * n2 == n and n2 >= SUBLANES, (n_tok, n1, n2)
    k1 = n1 // 2 + 1
    k1p = -(-k1 // BF16_ROWS) * BF16_ROWS
    return n, n1, n2, k1, k1p


def _dft_constants(n_tok):
    n, n1, n2, k1, k1p = _dft_plan(n_tok)
    two_pi = 2.0 * np.pi
    kk = np.arange(k1p)[:, None]
    valid = (kk < k1).astype(np.float64)
    ang = two_pi * ((kk * np.arange(n1)[None, :]) % n1) / n1
    f1 = np.concatenate([np.cos(ang) * valid, -np.sin(ang) * valid], axis=0)
    a2 = two_pi * ((np.arange(n2)[:, None] * np.arange(n2)[None, :]) % n2) / n2
    c2, s2 = np.cos(a2), np.sin(a2)
    f2 = np.block([[c2, s2], [-s2, c2]])
    g2 = np.block([[c2, -s2], [s2, c2]])
    wk = np.where((np.arange(k1p) == 0) | (np.arange(k1p) == n1 // 2), 1.0, 2.0) * (np.arange(k1p) < k1)
    a3 = two_pi * ((np.arange(n1 // 2)[:, None] * np.arange(k1p)[None, :]) % n1) / n1
    g3 = np.concatenate([np.cos(a3) * wk[None, :], -np.sin(a3) * wk[None, :]], axis=1) / n
    at = two_pi * (kk * np.arange(n2)[None, :]) / n
    return dict(
        f1_full=_mat3(f1), f1_half=_mat3(f1[:, :n1 // 2]), f2=_mat3(f2), g2=_mat3(g2), g3=_mat3(g3),
        tw_cos=jnp.asarray(np.cos(at), F32), tw_sin=jnp.asarray(np.sin(at), F32),
        tw_cos_t=jnp.asarray(np.cos(at).T, F32), tw_sin_t=jnp.asarray(np.sin(at).T, F32),
    )


def _table_column(tab_ref, idx):
    lane = lax.broadcasted_iota(jnp.int32, tab_ref.shape, 1)
    return jnp.sum(jnp.where(lane == idx, tab_ref[...], 0.0), axis=1, keepdims=True)


def _fft1_kernel(x_ref, f_ref, tc_ref, ts_ref, are_ref, aim_ref, *, k1p, blocks_per_n2):
    n2 = pl.program_id(0) // blocks_per_n2
    a = jnp.dot(f_ref[...], _split3(x_ref[...]), preferred_element_type=F32)
    re, im = a[:k1p], a[k1p:]
    c = _table_column(tc_ref, n2)
    s = _table_column(ts_ref, n2)
    are_ref[...] = re * c + im * s
    aim_ref[...] = im * c - re * s


def _fft_stage1(x2d, f1, consts, n_tok, n_ch):
    _, n1, n2, _, k1p = _dft_plan(n_tok)
    n1c, cols = x2d.shape
    tcol = _pick(n_ch, 2048, LANES)
    tab = pl.BlockSpec((k1p, n2), lambda j: (0, 0))
    out = pl.BlockSpec((k1p, tcol), lambda j: (0, j))
    return pl.pallas_call(
        functools.partial(_fft1_kernel, k1p=k1p, blocks_per_n2=n_ch // tcol),
        grid=(cols // tcol,),
        in_specs=[pl.BlockSpec((n1c, tcol), lambda j: (0, j)),
                  pl.BlockSpec((2 * k1p, 3 * n1c), lambda j: (0, 0)), tab, tab],
        out_specs=[out, out],
        out_shape=[jax.ShapeDtypeStruct((k1p, cols), F32)] * 2,
        compiler_params=_params("parallel"),
        name="hyena_fft1",
    )(x2d, f1, consts["tw_cos"], consts["tw_sin"])


def _fft2_spec_kernel(are_ref, aim_ref, f_ref, asum_ref, sre_ref, sim_ref, *, tk1, n2):
    inv = 1.0 / (asum_ref[...] + NORM_EPS)

    def body(r, carry):
        a = jnp.concatenate([are_ref[r], aim_ref[r]], axis=0)
        x = jnp.dot(f_ref[...], _split3(a), preferred_element_type=F32)
        sre_ref[r] = x[:n2] * inv
        sim_ref[r] = x[n2:] * inv
        return carry

    lax.fori_loop(0, tk1, body, 0)


def _fft2_conv_kernel(are_ref, aim_ref, sre_ref, sim_ref, f_ref, g_ref, tc_ref, ts_ref,
                      bre_ref, bim_ref, *, tk1, n2):
    i = pl.program_id(0)

    def body(r, carry):
        a = jnp.concatenate([are_ref[r], aim_ref[r]], axis=0)
        x = jnp.dot(f_ref[...], _split3(a), preferred_element_type=F32)
        xr, xi = x[:n2], x[n2:]
        sr, si = sre_ref[r], sim_ref[r]
        y = jnp.concatenate([xr * sr - xi * si, xr * si + xi * sr], axis=0)
        b = jnp.dot(g_ref[...], _split3(y), preferred_element_type=F32)
        br, bi = b[:n2], b[n2:]
        c = _table_column(tc_ref, i * tk1 + r)
        s = _table_column(ts_ref, i * tk1 + r)
        bre_ref[r] = br * c - bi * s
        bim_ref[r] = bi * c + br * s
        return carry

    lax.fori_loop(0, tk1, body, 0)


def _fft_stage2(are, aim, consts, n_tok, n_ch, *, spec=None, asum=None, spec_offset=0):
    _, n1, n2, _, k1p = _dft_plan(n_tok)
    tk1 = SUBLANES
    td = _pick(n_ch, 512, LANES)
    blk = pl.BlockSpec((tk1, n2, td), lambda i, j: (i, 0, j))
    mat = pl.BlockSpec((2 * n2, 6 * n2), lambda i, j: (0, 0))
    grid = (k1p // tk1, n_ch // td)
    out_shape = [jax.ShapeDtypeStruct((k1p, n2, n_ch), F32)] * 2
    if spec is None:
        return pl.pallas_call(
            functools.partial(_fft2_spec_kernel, tk1=tk1, n2=n2),
            grid=grid,
            in_specs=[blk, blk, mat, pl.BlockSpec((1, td), lambda i, j: (0, j))],
            out_specs=[blk, blk],
            out_shape=out_shape,
            compiler_params=_params("parallel", "parallel"),
            name="hyena_fft2_spec",
        )(are, aim, consts["f2"], asum)
    off = spec_offset // td
    sblk = pl.BlockSpec((tk1, n2, td), lambda i, j: (i, 0, j + off))
    tab = pl.BlockSpec((n2, k1p), lambda i, j: (0, 0))
    return pl.pallas_call(
        functools.partial(_fft2_conv_kernel, tk1=tk1, n2=n2),
        grid=grid,
        in_specs=[blk, blk, sblk, sblk, mat, mat, tab, tab],
        out_specs=[blk, blk],
        out_shape=out_shape,
        compiler_params=_params("parallel", "parallel"),
        name="hyena_fft2_conv",
    )(are, aim, spec[0], spec[1], consts["f2"], consts["g2"], consts["tw_cos_t"], consts["tw_sin_t"])


def _fft3_kernel(bre_ref, bim_ref, g_ref, y_ref, gate_ref, bias_ref, o_ref):
    b = jnp.concatenate([bre_ref[...], bim_ref[...]], axis=0)
    conv = jnp.dot(g_ref[...], _split3(b), preferred_element_type=F32)
    o_ref[...] = (gate_ref[...] * (conv + y_ref[...] * bias_ref[...])).astype(o_ref.dtype)


def _fft_stage3(bre, bim, consts, n_tok, n_ch, y2d, gate2d, bias, out_dtype):
    _, n1, n2, _, k1p = _dft_plan(n_tok)
    cols = n2 * n_ch
    tcol = _pick(n_ch, 2048, LANES)
    per = n_ch // tcol
    inb = pl.BlockSpec((k1p, tcol), lambda j: (0, j))
    sig = pl.BlockSpec((n1 // 2, tcol), lambda j: (0, j))
    return pl.pallas_call(
        _fft3_kernel,
        grid=(cols // tcol,),
        in_specs=[inb, inb, pl.BlockSpec((n1 // 2, 6 * k1p), lambda j: (0, 0)), sig, sig,
                  pl.BlockSpec((1, tcol), lambda j: (0, j % per))],
        out_specs=sig,
        out_shape=jax.ShapeDtypeStruct((n1 // 2, cols), out_dtype),
        compiler_params=_params("parallel"),
        name="hyena_fft3",
    )(bre, bim, consts["g3"], y2d, gate2d, bias.reshape(1, n_ch))


def _hyena_mixer_core(h, w_in, b_in, conv_w, conv_b, f_w1, f_b1, f_w2, f_b2, f_w3, f_freq, f_bias):
    n_tok, d = h.shape
    _, n1, n2, _, k1p = _dft_plan(n_tok)
    consts = _dft_constants(n_tok)
    z = _matmul(h, (w_in,), "bias", F32, extra=(b_in,))
    zs = _short_conv(z, conv_w, conv_b, HY_ORDER + 1)

    c = HY_ORDER * d
    taps, asum = _hyena_filter_taps(n_tok, d, f_w1, f_b1, f_w2, f_b2, f_w3, f_freq)
    tre, tim = _fft_stage1(taps.reshape(n1, n2 * c), consts["f1_full"], consts, n_tok, c)
    spec = _fft_stage2(tre.reshape(k1p, n2, c), tim.reshape(k1p, n2, c), consts, n_tok, c, asum=asum)

    y = zs[0].reshape(n1 // 2, n2 * d)
    for n in range(HY_ORDER):
        are, aim = _fft_stage1(y, consts["f1_half"], consts, n_tok, d)
        bre, bim = _fft_stage2(are.reshape(k1p, n2, d), aim.reshape(k1p, n2, d), consts, n_tok, d,
                               spec=spec, spec_offset=n * d)
        last = n == HY_ORDER - 1
        y = _fft_stage3(bre.reshape(k1p, n2 * d), bim.reshape(k1p, n2 * d), consts, n_tok, d, y,
                        zs[n + 1].reshape(n1 // 2, n2 * d), f_bias[n], BF16 if last else F32)
    return y.reshape(n_tok, d)


def _axial_rope_tables(n_tok):
    rows = n_tok // GRID_W
    row = jnp.repeat(jnp.arange(rows, dtype=F32), GRID_W)
    col = (jnp.arange(rows * GRID_W) % GRID_W).astype(F32)
    n_pair = HEAD_DIM // 4
    inv_freq = ROPE_THETA ** (-jnp.arange(n_pair, dtype=F32) / n_pair)
    ang = jnp.concatenate([row[:, None] * inv_freq, col[:, None] * inv_freq], axis=-1)
    cos, sin = jnp.cos(ang), jnp.sin(ang)
    return jnp.concatenate([cos, cos], axis=-1), jnp.concatenate([-sin, sin], axis=-1)


def _split_half_cols(w):
    k, n = w.shape
    return w.reshape(k, n // HEAD_DIM, HEAD_DIM // 2, 2).transpose(0, 1, 3, 2).reshape(k, n)


def _split_half_vec(g):
    return g.reshape(HEAD_DIM // 2, 2).T.reshape(HEAD_DIM)


def kernel(x, c, ctx, c_ctx, ada_a, ada_b, ada_bias, norm1_g, norm2_g, ffn_w_gate, ffn_w_up, ffn_w_down,
           hy_w_in, hy_b_in, hy_conv_w, hy_conv_b, hy_f_w1, hy_f_b1, hy_f_w2, hy_f_b2, hy_f_w3, hy_f_freq,
           hy_f_bias, hy_w_out, gqa_w_q, gqa_w_k, gqa_w_v, gqa_q_norm, gqa_k_norm, gqa_w_o,
           diff_w_q, diff_w_k, diff_w_v, diff_q_norm, diff_k_norm, diff_lambda_q1, diff_lambda_k1,
           diff_lambda_q2, diff_lambda_k2, diff_sub_norm, diff_w_o):
    assert x.shape[0] == 1 and ctx.shape[0] == 1
    depth = ada_a.shape[0]
    n_tok, d = x.shape[1], x.shape[2]
    xs = x[0]
    cs = ctx[0]
    cos, sin = _axial_rope_tables(n_tok)

    cc = jnp.concatenate([c.reshape(1, d), c_ctx.reshape(1, d), jnp.zeros((SUBLANES - 2, d), F32)], axis=0)
    mods = _ada_mods(cc, ada_a, ada_b, ada_bias)

    kinds = [i % N_MIXERS for i in range(depth)]
    for i in range(depth):
        kind = kinds[i]
        j = i // N_MIXERS
        reads_ctx = kind != MIX_HYENA
        update_ctx = any(kk != MIX_HYENA for kk in kinds[i + 1:])
        sh1, sc1, g1, sh2, sc2, g2 = [mods[i, 0, m * d:(m + 1) * d] for m in range(N_MOD)]
        csh1, csc1, cg1, csh2, csc2, cg2 = [mods[i, 1, m * d:(m + 1) * d] for m in range(N_MOD)]
        h = _norm_mod(xs, norm1_g[i], sh1, sc1)
        hc = _norm_mod(cs, norm1_g[i], csh1, csc1) if (reads_ctx or update_ctx) else None
        yc = None
        if kind == MIX_HYENA:
            w_in = hy_w_in[j].astype(BF16)
            w_out = hy_w_out[j].astype(BF16)
            args = (w_in, hy_b_in[j], hy_conv_w[j], hy_conv_b[j], hy_f_w1[j], hy_f_b1[j], hy_f_w2[j],
                    hy_f_b2[j], hy_f_w3[j], hy_f_freq[j], hy_f_bias[j])
            y = _hyena_mixer_core(h, *args)
            xs = _matmul(y, (w_out,), "resid", F32, extra=(xs, g1))
            if update_ctx:
                yc = _hyena_mixer_core(hc, *args)
                cs = _matmul(yc, (w_out,), "resid", F32, extra=(cs, cg1))
        elif kind == MIX_GQA:
            w_q = _split_half_cols(gqa_w_q[j]).astype(BF16)
            w_k = _split_half_cols(gqa_w_k[j]).astype(BF16)
            w_v = gqa_w_v[j].astype(BF16)
            w_o = gqa_w_o[j].astype(BF16)
            qg = _split_half_vec(gqa_q_norm[j])
            kg = _split_half_vec(gqa_k_norm[j])
            q = _matmul(h, (w_q,), "headrope", BF16, extra=(qg, cos, sin), out_scale=ATTN_Q_SCALE)
            k = _matmul(h, (w_k,), "headrope", BF16, extra=(kg, cos, sin))
            v = _matmul(h, (w_v,), "plain", BF16)
            kc = _matmul(hc, (w_k,), "headnorm", BF16, extra=(kg,))
            vc = _matmul(hc, (w_v,), "plain", BF16)
            o = _attention(q, jnp.concatenate([k, kc], axis=0), jnp.concatenate([v, vc], axis=0), "gqa")
            xs = _matmul(o, (w_o,), "resid", F32, extra=(xs, g1))
            if update_ctx:
                qc = _matmul(hc, (w_q,), "headnorm", BF16, extra=(qg,), out_scale=ATTN_Q_SCALE)
                oc = _attention(qc, kc, vc, "gqa")
                cs = _matmul(oc, (w_o,), "resid", F32, extra=(cs, cg1))
        else:
            w_q = _split_half_cols(diff_w_q[j]).astype(BF16)
            w_k = _split_half_cols(diff_w_k[j]).astype(BF16)
            w_v = diff_w_v[j].astype(BF16)
            w_o = diff_w_o[j].astype(BF16)
            qg = _split_half_vec(diff_q_norm[j])
            kg = _split_half_vec(diff_k_norm[j])
            lam_init = 0.8 - 0.6 * math.exp(-0.3 * i)
            lam_params = (diff_lambda_q1[j], diff_lambda_k1[j], diff_lambda_q2[j], diff_lambda_k2[j])
            attend = functools.partial(_attention, kind="diff", lam_params=lam_params,
                                       sub_gain=diff_sub_norm[j], lam_init=lam_init)
            q = _matmul(h, (w_q,), "headrope", BF16, extra=(qg, cos, sin), out_scale=ATTN_Q_SCALE)
            k = _matmul(h, (w_k,), "headrope", BF16, extra=(kg, cos, sin))
            v = _matmul(h, (w_v,), "plain", BF16)
            kc = _matmul(hc, (w_k,), "headnorm", BF16, extra=(kg,))
            vc = _matmul(hc, (w_v,), "plain", BF16)
            o = attend(q, jnp.concatenate([k, kc], axis=0), jnp.concatenate([v, vc], axis=0))
            xs = _matmul(o, (w_o,), "resid", F32, extra=(xs, g1))
            if update_ctx:
                qc = _matmul(hc, (w_q,), "headnorm", BF16, extra=(qg,), out_scale=ATTN_Q_SCALE)
                oc = attend(qc, kc, vc)
                cs = _matmul(oc, (w_o,), "resid", F32, extra=(cs, cg1))

        w_gate = ffn_w_gate[i].astype(BF16)
        w_up = ffn_w_up[i].astype(BF16)
        w_down = ffn_w_down[i].astype(BF16)
        h2 = _norm_mod(xs, norm2_g[i], sh2, sc2)
        act = _matmul(h2, (w_gate, w_up), "swiglu", BF16, tn=256)
        xs = _matmul(act, (w_down,), "resid", F32, extra=(xs, g2), tm=512, tn=256)
        if update_ctx:
            hc2 = _norm_mod(cs, norm2_g[i], csh2, csc2)
            actc = _matmul(hc2, (w_gate, w_up), "swiglu", BF16, tn=256)
            cs = _matmul(actc, (w_down,), "resid", F32, extra=(cs, cg2), tm=512, tn=256)
    return xs[None]
```

```python
import functools
import math

import numpy as np
import jax
import jax.numpy as jnp
from jax import lax
from jax.experimental import pallas as pl
from jax.experimental.pallas import tpu as pltpu

HEAD_DIM = 128
GRID_W = 64
GQA_GROUP = 4
N_MOD = 6
N_MIXERS = 3
MIX_HYENA, MIX_GQA, MIX_DIFF = 0, 1, 2
NORM_EPS = 1e-6
ROPE_THETA = 10000.0
HY_ORDER = 2
HY_EMB_DIM = 33
HY_DECAY_TARGET = 1e-2
HY_FAST_DECAY_PCT = 0.3
HY_SLOW_DECAY_PCT = 1.5
ATTN_Q_SCALE = HEAD_DIM ** -0.5 * math.log2(math.e)
V7X_MXU_WIDTH = 256
ATTN_COL_TILE = V7X_MXU_WIDTH
V7X_VMEM_BYTES = 64 * 1024 * 1024
VMEM_LIMIT_BYTES = V7X_VMEM_BYTES - 8 * 1024 * 1024
LANES = 128
SUBLANES = 8
BF16_ROWS = 16
MM_ROW_CHUNK = 256

F32 = jnp.float32
BF16 = jnp.bfloat16
HIGHEST = lax.Precision.HIGHEST


def _pick(dim, pref, align):
    if dim <= pref:
        return dim
    best = None
    for d in range(align, pref + 1, align):
        if dim % d == 0:
            best = d
    assert best is not None, (dim, pref, align)
    return best


def _params(*sem):
    return pltpu.CompilerParams(dimension_semantics=sem, vmem_limit_bytes=VMEM_LIMIT_BYTES)


def _silu(x):
    return x * (1.0 / (1.0 + jnp.exp(-x)))


def _ada_kernel(cc_ref, a_ref, b_ref, bias_ref, o_ref):
    s = _silu(cc_ref[...])
    t = jnp.dot(s, a_ref[0], precision=HIGHEST, preferred_element_type=F32)
    o_ref[0] = jnp.dot(t, b_ref[0], precision=HIGHEST, preferred_element_type=F32) + bias_ref[0]


def _ada_mods(cc, ada_a, ada_b, ada_bias):
    depth, d, r = ada_a.shape
    n = ada_b.shape[2]
    tn = _pick(n, 2048, LANES)
    return pl.pallas_call(
        _ada_kernel,
        grid=(depth, n // tn),
        in_specs=[
            pl.BlockSpec((SUBLANES, d), lambda l, j: (0, 0)),
            pl.BlockSpec((1, d, r), lambda l, j: (l, 0, 0)),
            pl.BlockSpec((1, r, tn), lambda l, j: (l, 0, j)),
            pl.BlockSpec((1, 1, tn), lambda l, j: (l, 0, j)),
        ],
        out_specs=pl.BlockSpec((1, SUBLANES, tn), lambda l, j: (l, 0, j)),
        out_shape=jax.ShapeDtypeStruct((depth, SUBLANES, n), F32),
        compiler_params=_params("parallel", "parallel"),
        name="ada_mods",
    )(cc, ada_a, ada_b, ada_bias.reshape(depth, 1, n))


def _norm_mod_kernel(x_ref, g_ref, sh_ref, sc_ref, o_ref):
    x = x_ref[...]
    ms = jnp.mean(x * x, axis=-1, keepdims=True)
    y = x * lax.rsqrt(ms + NORM_EPS) * g_ref[...]
    o_ref[...] = (y * (1.0 + sc_ref[...]) + sh_ref[...]).astype(o_ref.dtype)


def _norm_mod(x, gain, shift, scale):
    m, d = x.shape
    tr = _pick(m, 256, SUBLANES)
    vec = pl.BlockSpec((1, d), lambda i: (0, 0))
    return pl.pallas_call(
        _norm_mod_kernel,
        grid=(m // tr,),
        in_specs=[pl.BlockSpec((tr, d), lambda i: (i, 0)), vec, vec, vec],
        out_specs=pl.BlockSpec((tr, d), lambda i: (i, 0)),
        out_shape=jax.ShapeDtypeStruct((m, d), BF16),
        compiler_params=_params("parallel"),
        name="norm_mod",
    )(x, gain.reshape(1, d), shift.reshape(1, d), scale.reshape(1, d))


def _head_norm_rope(y, gain, cos, sin, out_scale):
    outs = []
    for h in range(y.shape[1] // HEAD_DIM):
        t = y[:, h * HEAD_DIM:(h + 1) * HEAD_DIM]
        ms = jnp.mean(t * t, axis=-1, keepdims=True)
        t = t * lax.rsqrt(ms + NORM_EPS) * gain
        if cos is not None:
            t = t * cos + pltpu.roll(t, HEAD_DIM // 2, axis=1) * sin
        if out_scale != 1.0:
            t = t * out_scale
        outs.append(t)
    return outs[0] if len(outs) == 1 else jnp.concatenate(outs, axis=1)


def _mm_kernel(*refs, mode, nk, out_scale):
    n_w = 2 if mode == "swiglu" else 1
    a_ref, w_refs = refs[0], refs[1:1 + n_w]
    n_extra = {"bias": 1, "plain": 0, "swiglu": 0, "resid": 2, "headnorm": 1, "headrope": 3}[mode]
    extra = refs[1 + n_w:1 + n_w + n_extra]
    o_ref = refs[1 + n_w + n_extra]
    accs = refs[2 + n_w + n_extra:]
    k = pl.program_id(2)

    def epilogue(ys, rows=slice(None)):
        y = ys[0]
        if mode == "bias":
            y = y + extra[0][...]
        elif mode == "swiglu":
            y = _silu(y) * ys[1]
        elif mode == "resid":
            y = extra[0][rows, :] + extra[1][...] * y
        elif mode == "headnorm":
            y = _head_norm_rope(y, extra[0][...], None, None, out_scale)
        elif mode == "headrope":
            y = _head_norm_rope(y, extra[0][...], extra[1][rows, :], extra[2][rows, :], out_scale)
        o_ref[rows, :] = y.astype(o_ref.dtype)

    if nk == 1:
        tm = a_ref.shape[0]
        rc = MM_ROW_CHUNK if tm % MM_ROW_CHUNK == 0 else tm
        pending = None
        for r in range(0, tm, rc):
            rows = slice(r, r + rc)
            a = a_ref[rows, :]
            parts = [jnp.dot(a, w[...], preferred_element_type=F32) for w in w_refs]
            if pending is not None:
                epilogue(*pending)
            pending = (parts, rows)
        epilogue(*pending)
        return

    a = a_ref[...]
    parts = [jnp.dot(a, w[...], preferred_element_type=F32) for w in w_refs]

    @pl.when(k == 0)
    def _():
        for acc, p in zip(accs, parts):
            acc[...] = p

    @pl.when(jnp.logical_and(k > 0, k < nk - 1))
    def _():
        for acc, p in zip(accs, parts):
            acc[...] += p

    @pl.when(k == nk - 1)
    def _():
        epilogue([acc[...] + p for acc, p in zip(accs, parts)])


def _matmul(a, ws, mode, out_dtype, *, extra=(), tm=1024, tn=512, tk=None, out_scale=1.0):
    m, kdim = a.shape
    n = ws[0].shape[1]
    tm = _pick(m, tm, BF16_ROWS)
    tn = _pick(n, tn, LANES)
    tk = kdim if tk is None else _pick(kdim, tk, LANES)
    nk = kdim // tk
    grid = (m // tm, n // tn, nk)
    in_specs = [pl.BlockSpec((tm, tk), lambda i, j, k: (i, k))]
    in_specs += [pl.BlockSpec((tk, tn), lambda i, j, k: (k, j)) for _ in ws]
    row = pl.BlockSpec((1, tn), lambda i, j, k: (0, j))
    head = pl.BlockSpec((1, HEAD_DIM), lambda i, j, k: (0, 0))
    ops = []
    if mode == "bias":
        in_specs += [row]
        ops = [extra[0].reshape(1, n)]
    elif mode == "resid":
        in_specs += [pl.BlockSpec((tm, tn), lambda i, j, k: (i, j)), row]
        ops = [extra[0], extra[1].reshape(1, n)]
    elif mode == "headnorm":
        in_specs += [head]
        ops = [extra[0].reshape(1, HEAD_DIM)]
    elif mode == "headrope":
        tab = pl.BlockSpec((tm, HEAD_DIM), lambda i, j, k: (i, 0))
        in_specs += [head, tab, tab]
        ops = [extra[0].reshape(1, HEAD_DIM), extra[1], extra[2]]
    n_acc = len(ws) if nk > 1 else 0
    return pl.pallas_call(
        functools.partial(_mm_kernel, mode=mode, nk=nk, out_scale=out_scale),
        grid=grid,
        in_specs=in_specs,
        out_specs=pl.BlockSpec((tm, tn), lambda i, j, k: (i, j)),
        out_shape=jax.ShapeDtypeStruct((m, n), out_dtype),
        scratch_shapes=[pltpu.VMEM((tm, tn), F32) for _ in range(n_acc)],
        compiler_params=_params("parallel", "parallel", "arbitrary"),
        name="mm_" + mode,
    )(a, *ws, *ops)


def _attn_kernel(*refs, kind, n_kv, lam_init):
    if kind == "gqa":
        q_ref, k_ref, vt_ref, o_ref, m_sc, acc_sc = refs
        n_q = GQA_GROUP
    else:
        q_ref, k_ref, vt_ref, lq1, lk1, lq2, lk2, sg_ref, o_ref, m_sc, acc_sc = refs
        n_q = 2
    kv = pl.program_id(2)
    dv = vt_ref.shape[0] - BF16_ROWS

    @pl.when(kv == 0)
    def _():
        m_sc[...] = jnp.full(m_sc.shape, -jnp.inf, F32)
        acc_sc[...] = jnp.zeros(acc_sc.shape, F32)

    vt = vt_ref[...]
    tq = q_ref.shape[0]
    tc = min(tq, ATTN_COL_TILE)
    units = [(h, c) for h in range(n_q) for c in range(tq // tc)]

    def scores_t(u):
        h, c = units[u]
        q = q_ref[c * tc:(c + 1) * tc, h * HEAD_DIM:(h + 1) * HEAD_DIM]
        kh = 0 if kind == "gqa" else h
        k = k_ref[:, kh * HEAD_DIM:(kh + 1) * HEAD_DIM]
        return lax.dot_general(k, q, (((1,), (1,)), ((), ())), preferred_element_type=F32)

    def softmax_t(u, st):
        h, c = units[u]
        m_prev = m_sc[h, :, c * tc:(c + 1) * tc]
        m_new = jnp.maximum(m_prev, jnp.max(st, axis=0, keepdims=True))
        m_sc[h, :, c * tc:(c + 1) * tc] = m_new
        return jnp.exp2(st - m_new).astype(vt.dtype), jnp.exp2(m_prev - m_new)

    def accumulate(u, p, alpha):
        h, c = units[u]
        acc = acc_sc[h, :, c * tc:(c + 1) * tc]
        acc_sc[h, :, c * tc:(c + 1) * tc] = alpha * acc + jnp.dot(vt, p, preferred_element_type=F32)

    st, pa = {}, {}
    for i in range(len(units) + 2):
        if i < len(units):
            st[i] = scores_t(i)
        if 0 <= i - 1 < len(units):
            pa[i - 1] = softmax_t(i - 1, st.pop(i - 1))
        if 0 <= i - 2 < len(units):
            accumulate(i - 2, *pa.pop(i - 2))

    @pl.when(kv == n_kv - 1)
    def _():
        def normalised(h):
            acc = acc_sc[h]
            return acc[:dv] / acc[dv:dv + 1]

        if kind == "gqa":
            for h in range(n_q):
                o_ref[:, h * HEAD_DIM:(h + 1) * HEAD_DIM] = normalised(h).T.astype(o_ref.dtype)
        else:
            lam = (jnp.exp(jnp.sum(lq1[...] * lk1[...], axis=-1, keepdims=True))
                   - jnp.exp(jnp.sum(lq2[...] * lk2[...], axis=-1, keepdims=True)) + lam_init)
            o = (normalised(0) - lam * normalised(1)).T
            ms = jnp.mean(o * o, axis=-1, keepdims=True)
            o = o * lax.rsqrt(ms + NORM_EPS) * sg_ref[...] * (1.0 - lam_init)
            o_ref[...] = o.astype(o_ref.dtype)


def _attention(q, k, v, kind, *, lam_params=None, sub_gain=None, lam_init=0.0):
    lq, d = q.shape
    t = k.shape[0]
    if kind == "gqa":
        n_heads = d // (GQA_GROUP * HEAD_DIM)
        qw, kw, vw, n_q = GQA_GROUP * HEAD_DIM, HEAD_DIM, HEAD_DIM, GQA_GROUP
    else:
        n_heads = d // (2 * HEAD_DIM)
        qw, kw, vw, n_q = 2 * HEAD_DIM, 2 * HEAD_DIM, 2 * HEAD_DIM, 2
    tq = _pick(lq, 1024, ATTN_COL_TILE)
    tk = _pick(t, 1280, LANES)
    n_kv = t // tk
    vt = v.T.reshape(v.shape[1] // vw, vw, t)
    ones = jnp.zeros((vt.shape[0], BF16_ROWS, t), v.dtype).at[:, 0, :].set(1.0)
    vt = jnp.concatenate([vt, ones], axis=1)
    vr = vw + BF16_ROWS
    in_specs = [
        pl.BlockSpec((tq, qw), lambda h, i, j: (i, h)),
        pl.BlockSpec((tk, kw), lambda h, i, j: (j, h)),
        pl.BlockSpec((None, vr, tk), lambda h, i, j: (h, 0, j)),
    ]
    ops = [q, k, vt]
    if kind == "diff":
        vec = pl.BlockSpec((1, HEAD_DIM), lambda h, i, j: (0, 0))
        in_specs += [vec, vec, vec, vec, pl.BlockSpec((1, 2 * HEAD_DIM), lambda h, i, j: (0, 0))]
        ops += [p.reshape(1, HEAD_DIM) for p in lam_params] + [sub_gain.reshape(1, 2 * HEAD_DIM)]
    return pl.pallas_call(
        functools.partial(_attn_kernel, kind=kind, n_kv=n_kv, lam_init=lam_init),
        grid=(n_heads, lq // tq, n_kv),
        in_specs=in_specs,
        out_specs=pl.BlockSpec((tq, qw), lambda h, i, j: (i, h)),
        out_shape=jax.ShapeDtypeStruct((lq, d), BF16),
        scratch_shapes=[
            pltpu.VMEM((n_q, 1, tq), F32),
            pltpu.VMEM((n_q, vr, tq), F32),
        ],
        compiler_params=_params("parallel", "parallel", "arbitrary"),
        name="attn_" + kind,
    )(*ops)


def _sconv_kernel(z_ref, zp_ref, zn_ref, w_ref, b_ref, o_ref, *, n_row_blocks):
    i = pl.program_id(1)
    z = z_ref[...]
    tr = z.shape[0]
    prev_row = jnp.where(i == 0, 0.0, zp_ref[SUBLANES - 1:SUBLANES, :])
    next_row = jnp.where(i == n_row_blocks - 1, 0.0, zn_ref[0:1, :])
    row = lax.broadcasted_iota(jnp.int32, z.shape, 0)
    z_before = jnp.where(row == 0, prev_row, pltpu.roll(z, 1, axis=0))
    z_after = jnp.where(row == tr - 1, next_row, pltpu.roll(z, tr - 1, axis=0))
    w = w_ref[...]
    o_ref[0] = b_ref[...] + z_before * w[0:1] + z * w[1:2] + z_after * w[2:3]


def _short_conv(z, conv_w, conv_b, n_split):
    l, c = z.shape
    d = c // n_split
    tr = _pick(l, 512, SUBLANES)
    tc = _pick(d, 1024, LANES)
    ncb = d // tc
    nrb = l // tr
    r8 = tr // SUBLANES
    last8 = l // SUBLANES - 1
    return pl.pallas_call(
        functools.partial(_sconv_kernel, n_row_blocks=nrb),
        grid=(n_split, nrb, ncb),
        in_specs=[
            pl.BlockSpec((tr, tc), lambda s, i, j: (i, s * ncb + j)),
            pl.BlockSpec((SUBLANES, tc), lambda s, i, j: (jnp.maximum(i * r8 - 1, 0), s * ncb + j)),
            pl.BlockSpec((SUBLANES, tc), lambda s, i, j: (jnp.minimum((i + 1) * r8, last8), s * ncb + j)),
            pl.BlockSpec((conv_w.shape[0], tc), lambda s, i, j: (0, s * ncb + j)),
            pl.BlockSpec((1, tc), lambda s, i, j: (0, s * ncb + j)),
        ],
        out_specs=pl.BlockSpec((1, tr, tc), lambda s, i, j: (s, i, j)),
        out_shape=jax.ShapeDtypeStruct((n_split, l, d), F32),
        compiler_params=_params("parallel", "parallel", "parallel"),
        name="short_conv",
    )(z, z, z, conv_w, conv_b.reshape(1, c))


def _filter_mlp_kernel(f_ref, w1_ref, b1_ref, w2_ref, b2_ref, fr_ref, o_ref):
    g = jnp.dot(f_ref[...], w1_ref[...], precision=HIGHEST, preferred_element_type=F32)
    g = jnp.sin(fr_ref[0:1] * (g + b1_ref[...]))
    g = jnp.dot(g, w2_ref[...], precision=HIGHEST, preferred_element_type=F32)
    o_ref[...] = jnp.sin(fr_ref[1:2] * (g + b2_ref[...]))


def _filter_taps_fft1_kernel(g_ref, w3f_ref, w3b_ref, dl_ref, f_ref, tc_ref, ts_ref,
                             are_ref, aim_ref, asum_ref, *, n_tok, n2_size, k1p):
    n2 = pl.program_id(1)
    n1_size = g_ref.shape[0]
    half = n1_size // 2
    g = g_ref[...]
    taps = jnp.concatenate([
        jnp.dot(g[:half], w3f_ref[...], precision=HIGHEST, preferred_element_type=F32),
        jnp.dot(g[half:], w3b_ref[...], precision=HIGHEST, preferred_element_type=F32)], axis=0)
    n = lax.broadcasted_iota(jnp.int32, (n1_size, 1), 0) * n2_size + n2
    tau = jnp.where(n < n_tok, n, 2 * n_tok - n)
    t = tau.astype(F32) * (1.0 / (n_tok - 1))
    taps = taps * jnp.exp(-t * dl_ref[...])
    taps = jnp.where(n == n_tok, 0.0, taps)

    @pl.when(n2 == 0)
    def _():
        asum_ref[...] = jnp.zeros(asum_ref.shape, F32)

    asum_ref[...] += jnp.sum(jnp.abs(taps), axis=0, keepdims=True)

    a = jnp.dot(f_ref[...], _split3(taps), preferred_element_type=F32)
    re, im = a[:k1p], a[k1p:]
    c = _table_column(tc_ref, n2)
    s = _table_column(ts_ref, n2)
    are_ref[...] = re * c + im * s
    aim_ref[...] = im * c - re * s


def _hyena_filter_stage1(n_tok, d, consts, f_w1, f_b1, f_w2, f_b2, f_w3, f_freq):
    width = f_w1.shape[1]
    t = jnp.linspace(0.0, 1.0, n_tok, dtype=F32)[:, None]
    bands = (HY_EMB_DIM - 1) // 2
    w = (2.0 * math.pi / n_tok) * jnp.arange(n_tok, dtype=F32)[:, None]
    f = jnp.linspace(1e-4, bands - 1, bands, dtype=F32)[None, :]
    feats = jnp.concatenate([t, jnp.cos(f * w), -jnp.sin(f * w)], axis=-1)
    feats2 = jnp.concatenate([feats, feats[:1], feats[:0:-1]], axis=0)
    feats2 = jnp.pad(feats2, ((0, 0), (0, LANES - HY_EMB_DIM)))
    pw = LANES - width
    w1 = jnp.pad(f_w1, ((0, LANES - HY_EMB_DIM), (0, pw)))
    w2 = jnp.pad(f_w2, ((0, pw), (0, pw)))
    b1 = jnp.pad(f_b1, (0, pw)).reshape(1, LANES)
    b2 = jnp.pad(f_b2, (0, pw)).reshape(1, LANES)
    freq = jnp.pad(f_freq, ((0, 0), (0, pw)))
    n2l = 2 * n_tok
    tr = _pick(n_tok, 512, SUBLANES)
    full = lambda shape: pl.BlockSpec(shape, lambda i: (0,) * len(shape))
    g = pl.pallas_call(
        _filter_mlp_kernel,
        grid=(n2l // tr,),
        in_specs=[pl.BlockSpec((tr, LANES), lambda i: (i, 0)), full((LANES, LANES)), full((1, LANES)),
                  full((LANES, LANES)), full((1, LANES)), full((2, LANES))],
        out_specs=pl.BlockSpec((tr, LANES), lambda i: (i, 0)),
        out_shape=jax.ShapeDtypeStruct((n2l, LANES), F32),
        compiler_params=_params("parallel"),
        name="hyena_filter_mlp",
    )(feats2, w1, b1, w2, b2, freq)

    c = HY_ORDER * d
    w3 = jnp.pad(f_w3, ((0, pw), (0, 0)))
    max_decay = math.log(HY_DECAY_TARGET) / HY_FAST_DECAY_PCT
    min_decay = math.log(HY_DECAY_TARGET) / HY_SLOW_DECAY_PCT
    deltas = jnp.abs(jnp.linspace(min_decay, max_decay, d, dtype=F32))
    deltas = jnp.tile(deltas, HY_ORDER).reshape(1, c)
    _, n1, n2, _, k1p = _dft_plan(n_tok)
    g_by_n2 = g.reshape(n1, n2, LANES).transpose(1, 0, 2)
    tcol = _pick(c, 2048, LANES)
    ncb = c // tcol
    tab = pl.BlockSpec((k1p, n2), lambda j, i: (0, 0))
    out = pl.BlockSpec((k1p, tcol), lambda j, i: (0, i * ncb + j))
    return pl.pallas_call(
        functools.partial(_filter_taps_fft1_kernel, n_tok=n_tok, n2_size=n2, k1p=k1p),
        grid=(ncb, n2),
        in_specs=[
            pl.BlockSpec((None, n1, LANES), lambda j, i: (i, 0, 0)),
            pl.BlockSpec((LANES, tcol), lambda j, i: (0, j)),
            pl.BlockSpec((LANES, tcol), lambda j, i: (0, ncb + j)),
            pl.BlockSpec((1, tcol), lambda j, i: (0, j)),
            pl.BlockSpec((2 * k1p, 3 * n1), lambda j, i: (0, 0)), tab, tab,
        ],
        out_specs=[out, out, pl.BlockSpec((1, tcol), lambda j, i: (0, j))],
        out_shape=[jax.ShapeDtypeStruct((k1p, n2 * c), F32)] * 2 + [jax.ShapeDtypeStruct((1, c), F32)],
        compiler_params=_params("parallel", "arbitrary"),
        name="hyena_filter_fft1",
    )(g_by_n2, w3, w3, deltas, consts["f1_full"], consts["tw_cos"], consts["tw_sin"])


def _split3(x):
    hi = x.astype(BF16)
    lo = (x - hi.astype(F32)).astype(BF16)
    return jnp.concatenate([hi, lo, hi], axis=0)


def _mat3(m64):
    m = jnp.asarray(m64, F32)
    hi = m.astype(BF16)
    lo = (m - hi.astype(F32)).astype(BF16)
    return jnp.concatenate([hi, hi, lo], axis=1)


def _dft_plan(n_tok):
    n = 2 * n_tok
    n1 = 1 << ((n.bit_length() - 1 + 1) // 2)
    n2 = n // n1
    assert n1 * n2 == n and n2 >= SUBLANES, (n_tok, n1, n2)
    k1 = n1 // 2 + 1
    k1p = -(-k1 // BF16_ROWS) * BF16_ROWS
    return n, n1, n2, k1, k1p


def _dft_constants(n_tok):
    n, n1, n2, k1, k1p = _dft_plan(n_tok)
    two_pi = 2.0 * np.pi
    kk = np.arange(k1p)[:, None]
    valid = (kk < k1).astype(np.float64)
    ang = two_pi * ((kk * np.arange(n1)[None, :]) % n1) / n1
    f1 = np.concatenate([np.cos(ang) * valid, -np.sin(ang) * valid], axis=0)
    a2 = two_pi * ((np.arange(n2)[:, None] * np.arange(n2)[None, :]) % n2) / n2
    c2, s2 = np.cos(a2), np.sin(a2)
    f2 = np.block([[c2, s2], [-s2, c2]])
    g2 = np.block([[c2, -s2], [s2, c2]])
    wk = np.where((np.arange(k1p) == 0) | (np.arange(k1p) == n1 // 2), 1.0, 2.0) * (np.arange(k1p) < k1)
    a3 = two_pi * ((np.arange(n1 // 2)[:, None] * np.arange(k1p)[None, :]) % n1) / n1
    g3 = np.concatenate([np.cos(a3) * wk[None, :], -np.sin(a3) * wk[None, :]], axis=1) / n
    at = two_pi * (kk * np.arange(n2)[None, :]) / n
    return dict(
        f1_full=_mat3(f1), f1_half=_mat3(f1[:, :n1 // 2]), f2=_mat3(f2), g2=_mat3(g2), g3=_mat3(g3),
        tw_cos=jnp.asarray(np.cos(at), F32), tw_sin=jnp.asarray(np.sin(at), F32),
        tw_cos_t=jnp.asarray(np.cos(at).T, F32), tw_sin_t=jnp.asarray(np.sin(at).T, F32),
    )


def _table_column(tab_ref, idx):
    lane = lax.broadcasted_iota(jnp.int32, tab_ref.shape, 1)
    return jnp.sum(jnp.where(lane == idx, tab_ref[...], 0.0), axis=1, keepdims=True)


def _fft1_kernel(x_ref, f_ref, tc_ref, ts_ref, are_ref, aim_ref, *, k1p, blocks_per_n2):
    n2 = pl.program_id(0) // blocks_per_n2
    a = jnp.dot(f_ref[...], _split3(x_ref[...]), preferred_element_type=F32)
    re, im = a[:k1p], a[k1p:]
    c = _table_column(tc_ref, n2)
    s = _table_column(ts_ref, n2)
    are_ref[...] = re * c + im * s
    aim_ref[...] = im * c - re * s


def _fft_stage1(x2d, f1, consts, n_tok, n_ch):
    _, n1, n2, _, k1p = _dft_plan(n_tok)
    n1c, cols = x2d.shape
    tcol = _pick(n_ch, 2048, LANES)
    tab = pl.BlockSpec((k1p, n2), lambda j: (0, 0))
    out = pl.BlockSpec((k1p, tcol), lambda j: (0, j))
    return pl.pallas_call(
        functools.partial(_fft1_kernel, k1p=k1p, blocks_per_n2=n_ch // tcol),
        grid=(cols // tcol,),
        in_specs=[pl.BlockSpec((n1c, tcol), lambda j: (0, j)),
                  pl.BlockSpec((2 * k1p, 3 * n1c), lambda j: (0, 0)), tab, tab],
        out_specs=[out, out],
        out_shape=[jax.ShapeDtypeStruct((k1p, cols), F32)] * 2,
        compiler_params=_params("parallel"),
        name="hyena_fft1",
    )(x2d, f1, consts["tw_cos"], consts["tw_sin"])


def _fft2_spec_kernel(are_ref, aim_ref, f_ref, asum_ref, sre_ref, sim_ref, *, tk1, n2):
    inv = 1.0 / (asum_ref[...] + NORM_EPS)

    def body(r, carry):
        a = jnp.concatenate([are_ref[r], aim_ref[r]], axis=0)
        x = jnp.dot(f_ref[...], _split3(a), preferred_element_type=F32)
        sre_ref[r] = x[:n2] * inv
        sim_ref[r] = x[n2:] * inv
        return carry

    lax.fori_loop(0, tk1, body, 0)


def _fft2_conv_kernel(are_ref, aim_ref, sre_ref, sim_ref, f_ref, g_ref, tc_ref, ts_ref,
                      bre_ref, bim_ref, *, tk1, n2):
    i = pl.program_id(0)

    def body(r, carry):
        a = jnp.concatenate([are_ref[r], aim_ref[r]], axis=0)
        x = jnp.dot(f_ref[...], _split3(a), preferred_element_type=F32)
        xr, xi = x[:n2], x[n2:]
        sr, si = sre_ref[r], sim_ref[r]
        y = jnp.concatenate([xr * sr - xi * si, xr * si + xi * sr], axis=0)
        b = jnp.dot(g_ref[...], _split3(y), preferred_element_type=F32)
        br, bi = b[:n2], b[n2:]
        c = _table_column(tc_ref, i * tk1 + r)
        s = _table_column(ts_ref, i * tk1 + r)
        bre_ref[r] = br * c - bi * s
        bim_ref[r] = bi * c + br * s
        return carry

    lax.fori_loop(0, tk1, body, 0)


def _fft_stage2(are, aim, consts, n_tok, n_ch, *, spec=None, asum=None, spec_offset=0):
    _, n1, n2, _, k1p = _dft_plan(n_tok)
    tk1 = SUBLANES
    td = _pick(n_ch, 512, LANES)
    blk = pl.BlockSpec((tk1, n2, td), lambda i, j: (i, 0, j))
    mat = pl.BlockSpec((2 * n2, 6 * n2), lambda i, j: (0, 0))
    grid = (k1p // tk1, n_ch // td)
    out_shape = [jax.ShapeDtypeStruct((k1p, n2, n_ch), F32)] * 2
    if spec is None:
        return pl.pallas_call(
            functools.partial(_fft2_spec_kernel, tk1=tk1, n2=n2),
            grid=grid,
            in_specs=[blk, blk, mat, pl.BlockSpec((1, td), lambda i, j: (0, j))],
            out_specs=[blk, blk],
            out_shape=out_shape,
            compiler_params=_params("parallel", "parallel"),
            name="hyena_fft2_spec",
        )(are, aim, consts["f2"], asum)
    off = spec_offset // td
    sblk = pl.BlockSpec((tk1, n2, td), lambda i, j: (i, 0, j + off))
    tab = pl.BlockSpec((n2, k1p), lambda i, j: (0, 0))
    return pl.pallas_call(
        functools.partial(_fft2_conv_kernel, tk1=tk1, n2=n2),
        grid=grid,
        in_specs=[blk, blk, sblk, sblk, mat, mat, tab, tab],
        out_specs=[blk, blk],
        out_shape=out_shape,
        compiler_params=_params("parallel", "parallel"),
        name="hyena_fft2_conv",
    )(are, aim, spec[0], spec[1], consts["f2"], consts["g2"], consts["tw_cos_t"], consts["tw_sin_t"])


def _fft3_kernel(bre_ref, bim_ref, g_ref, y_ref, gate_ref, bias_ref, o_ref):
    b = jnp.concatenate([bre_ref[...], bim_ref[...]], axis=0)
    conv = jnp.dot(g_ref[...], _split3(b), preferred_element_type=F32)
    o_ref[...] = (gate_ref[...] * (conv + y_ref[...] * bias_ref[...])).astype(o_ref.dtype)


def _fft_stage3(bre, bim, consts, n_tok, n_ch, y2d, gate2d, bias, out_dtype):
    _, n1, n2, _, k1p = _dft_plan(n_tok)
    cols = n2 * n_ch
    tcol = _pick(n_ch, 2048, LANES)
    per = n_ch // tcol
    inb = pl.BlockSpec((k1p, tcol), lambda j: (0, j))
    sig = pl.BlockSpec((n1 // 2, tcol), lambda j: (0, j))
    return pl.pallas_call(
        _fft3_kernel,
        grid=(cols // tcol,),
        in_specs=[inb, inb, pl.BlockSpec((n1 // 2, 6 * k1p), lambda j: (0, 0)), sig, sig,
                  pl.BlockSpec((1, tcol), lambda j: (0, j % per))],
        out_specs=sig,
        out_shape=jax.ShapeDtypeStruct((n1 // 2, cols), out_dtype),
        compiler_params=_params("parallel"),
        name="hyena_fft3",
    )(bre, bim, consts["g3"], y2d, gate2d, bias.reshape(1, n_ch))


def _hyena_mixer_core(h, w_in, b_in, conv_w, conv_b, f_w1, f_b1, f_w2, f_b2, f_w3, f_freq, f_bias):
    n_tok, d = h.shape
    _, n1, n2, _, k1p = _dft_plan(n_tok)
    consts = _dft_constants(n_tok)
    z = _matmul(h, (w_in,), "bias", F32, extra=(b_in,))
    zs = _short_conv(z, conv_w, conv_b, HY_ORDER + 1)

    c = HY_ORDER * d
    tre, tim, asum = _hyena_filter_stage1(n_tok, d, consts, f_w1, f_b1, f_w2, f_b2, f_w3, f_freq)
    spec = _fft_stage2(tre.reshape(k1p, n2, c), tim.reshape(k1p, n2, c), consts, n_tok, c, asum=asum)

    y = zs[0].reshape(n1 // 2, n2 * d)
    for n in range(HY_ORDER):
        are, aim = _fft_stage1(y, consts["f1_half"], consts, n_tok, d)
        bre, bim = _fft_stage2(are.reshape(k1p, n2, d), aim.reshape(k1p, n2, d), consts, n_tok, d,
                               spec=spec, spec_offset=n * d)
        last = n == HY_ORDER - 1
        y = _fft_stage3(bre.reshape(k1p, n2 * d), bim.reshape(k1p, n2 * d), consts, n_tok, d, y,
                        zs[n + 1].reshape(n1 // 2, n2 * d), f_bias[n], BF16 if last else F32)
    return y.reshape(n_tok, d)


def _axial_rope_tables(n_tok):
    rows = n_tok // GRID_W
    row = jnp.repeat(jnp.arange(rows, dtype=F32), GRID_W)
    col = (jnp.arange(rows * GRID_W) % GRID_W).astype(F32)
    n_pair = HEAD_DIM // 4
    inv_freq = ROPE_THETA ** (-jnp.arange(n_pair, dtype=F32) / n_pair)
    ang = jnp.concatenate([row[:, None] * inv_freq, col[:, None] * inv_freq], axis=-1)
    cos, sin = jnp.cos(ang), jnp.sin(ang)
    return jnp.concatenate([cos, cos], axis=-1), jnp.concatenate([-sin, sin], axis=-1)


def _split_half_cols(w):
    k, n = w.shape
    return w.reshape(k, n // HEAD_DIM, HEAD_DIM // 2, 2).transpose(0, 1, 3, 2).reshape(k, n)


def _split_half_vec(g):
    return g.reshape(HEAD_DIM // 2, 2).T.reshape(HEAD_DIM)


def kernel(x, c, ctx, c_ctx, ada_a, ada_b, ada_bias, norm1_g, norm2_g, ffn_w_gate, ffn_w_up, ffn_w_down,
           hy_w_in, hy_b_in, hy_conv_w, hy_conv_b, hy_f_w1, hy_f_b1, hy_f_w2, hy_f_b2, hy_f_w3, hy_f_freq,
           hy_f_bias, hy_w_out, gqa_w_q, gqa_w_k, gqa_w_v, gqa_q_norm, gqa_k_norm, gqa_w_o,
           diff_w_q, diff_w_k, diff_w_v, diff_q_norm, diff_k_norm, diff_lambda_q1, diff_lambda_k1,
           diff_lambda_q2, diff_lambda_k2, diff_sub_norm, diff_w_o):
    assert x.shape[0] == 1 and ctx.shape[0] == 1
    depth = ada_a.shape[0]
    n_tok, d = x.shape[1], x.shape[2]
    xs = x[0]
    cs = ctx[0]
    cos, sin = _axial_rope_tables(n_tok)

    cc = jnp.concatenate([c.reshape(1, d), c_ctx.reshape(1, d), jnp.zeros((SUBLANES - 2, d), F32)], axis=0)
    mods = _ada_mods(cc, ada_a, ada_b, ada_bias)

    kinds = [i % N_MIXERS for i in range(depth)]
    for i in range(depth):
        kind = kinds[i]
        j = i // N_MIXERS
        reads_ctx = kind != MIX_HYENA
        update_ctx = any(kk != MIX_HYENA for kk in kinds[i + 1:])
        sh1, sc1, g1, sh2, sc2, g2 = [mods[i, 0, m * d:(m + 1) * d] for m in range(N_MOD)]
        csh1, csc1, cg1, csh2, csc2, cg2 = [mods[i, 1, m * d:(m + 1) * d] for m in range(N_MOD)]
        h = _norm_mod(xs, norm1_g[i], sh1, sc1)
        hc = _norm_mod(cs, norm1_g[i], csh1, csc1) if (reads_ctx or update_ctx) else None
        yc = None
        if kind == MIX_HYENA:
            w_in = hy_w_in[j].astype(BF16)
            w_out = hy_w_out[j].astype(BF16)
            args = (w_in, hy_b_in[j], hy_conv_w[j], hy_conv_b[j], hy_f_w1[j], hy_f_b1[j], hy_f_w2[j],
                    hy_f_b2[j], hy_f_w3[j], hy_f_freq[j], hy_f_bias[j])
            y = _hyena_mixer_core(h, *args)
            xs = _matmul(y, (w_out,), "resid", F32, extra=(xs, g1))
            if update_ctx:
                yc = _hyena_mixer_core(hc, *args)
                cs = _matmul(yc, (w_out,), "resid", F32, extra=(cs, cg1))
        elif kind == MIX_GQA:
            w_q = _split_half_cols(gqa_w_q[j]).astype(BF16)
            w_k = _split_half_cols(gqa_w_k[j]).astype(BF16)
            w_v = gqa_w_v[j].astype(BF16)
            w_o = gqa_w_o[j].astype(BF16)
            qg = _split_half_vec(gqa_q_norm[j])
            kg = _split_half_vec(gqa_k_norm[j])
            q = _matmul(h, (w_q,), "headrope", BF16, extra=(qg, cos, sin), out_scale=ATTN_Q_SCALE)
            k = _matmul(h, (w_k,), "headrope", BF16, extra=(kg, cos, sin))
            v = _matmul(h, (w_v,), "plain", BF16)
            kc = _matmul(hc, (w_k,), "headnorm", BF16, extra=(kg,))
            vc = _matmul(hc, (w_v,), "plain", BF16)
            o = _attention(q, jnp.concatenate([k, kc], axis=0), jnp.concatenate([v, vc], axis=0), "gqa")
            xs = _matmul(o, (w_o,), "resid", F32, extra=(xs, g1))
            if update_ctx:
                qc = _matmul(hc, (w_q,), "headnorm", BF16, extra=(qg,), out_scale=ATTN_Q_SCALE)
                oc = _attention(qc, kc, vc, "gqa")
                cs = _matmul(oc, (w_o,), "resid", F32, extra=(cs, cg1))
        else:
            w_q = _split_half_cols(diff_w_q[j]).astype(BF16)
            w_k = _split_half_cols(diff_w_k[j]).astype(BF16)
            w_v = diff_w_v[j].astype(BF16)
            w_o = diff_w_o[j].astype(BF16)
            qg = _split_half_vec(diff_q_norm[j])
            kg = _split_half_vec(diff_k_norm[j])
            lam_init = 0.8 - 0.6 * math.exp(-0.3 * i)
            lam_params = (diff_lambda_q1[j], diff_lambda_k1[j], diff_lambda_q2[j], diff_lambda_k2[j])
            attend = functools.partial(_attention, kind="diff", lam_params=lam_params,
                                       sub_gain=diff_sub_norm[j], lam_init=lam_init)
            q = _matmul(h, (w_q,), "headrope", BF16, extra=(qg, cos, sin), out_scale=ATTN_Q_SCALE)
            k = _matmul(h, (w_k,), "headrope", BF16, extra=(kg, cos, sin))
            v = _matmul(h, (w_v,), "plain", BF16)
            kc = _matmul(hc, (w_k,), "headnorm", BF16, extra=(kg,))
            vc = _matmul(hc, (w_v,), "plain", BF16)
            o = attend(q, jnp.concatenate([k, kc], axis=0), jnp.concatenate([v, vc], axis=0))
            xs = _matmul(o, (w_o,), "resid", F32, extra=(xs, g1))
            if update_ctx:
                qc = _matmul(hc, (w_q,), "headnorm", BF16, extra=(qg,), out_scale=ATTN_Q_SCALE)
                oc = attend(qc, kc, vc)
                cs = _matmul(oc, (w_o,), "resid", F32, extra=(cs, cg1))

        w_gate = ffn_w_gate[i].astype(BF16)
        w_up = ffn_w_up[i].astype(BF16)
        w_down = ffn_w_down[i].astype(BF16)
        h2 = _norm_mod(xs, norm2_g[i], sh2, sc2)
        act = _matmul(h2, (w_gate, w_up), "swiglu", BF16, tn=256)
        xs = _matmul(act, (w_down,), "resid", F32, extra=(xs, g2), tm=512, tn=256)
        if update_ctx:
            hc2 = _norm_mod(cs, norm2_g[i], csh2, csc2)
            actc = _matmul(hc2, (w_gate, w_up), "swiglu", BF16, tn=256)
            cs = _matmul(actc, (w_down,), "resid", F32, extra=(cs, cg2), tm=512, tn=256)
    return xs[None]
```

```python
import functools
import math

import numpy as np
import jax
import jax.numpy as jnp
from jax import lax
from jax.experimental import pallas as pl
from jax.experimental.pallas import tpu as pltpu

HEAD_DIM = 128
GRID_W = 64
GQA_GROUP = 4
N_MOD = 6
N_MIXERS = 3
MIX_HYENA, MIX_GQA, MIX_DIFF = 0, 1, 2
NORM_EPS = 1e-6
ROPE_THETA = 10000.0
HY_ORDER = 2
HY_EMB_DIM = 33
HY_DECAY_TARGET = 1e-2
HY_FAST_DECAY_PCT = 0.3
HY_SLOW_DECAY_PCT = 1.5
ATTN_Q_SCALE = HEAD_DIM ** -0.5 * math.log2(math.e)
V7X_MXU_WIDTH = 256
ATTN_COL_TILE = V7X_MXU_WIDTH
ATTN_SOFTMAX_LAG = {"gqa": 3, "diff": 2}
ATTN_VALUE_LAG = {"gqa": 2, "diff": 1}
V7X_VMEM_BYTES = 64 * 1024 * 1024
VMEM_LIMIT_BYTES = V7X_VMEM_BYTES - 8 * 1024 * 1024
LANES = 128
SUBLANES = 8
BF16_ROWS = 16
MM_ROW_CHUNK = 256

F32 = jnp.float32
BF16 = jnp.bfloat16
HIGHEST = lax.Precision.HIGHEST


def _pick(dim, pref, align):
    if dim <= pref:
        return dim
    best = None
    for d in range(align, pref + 1, align):
        if dim % d == 0:
            best = d
    assert best is not None, (dim, pref, align)
    return best


def _params(*sem):
    return pltpu.CompilerParams(dimension_semantics=sem, vmem_limit_bytes=VMEM_LIMIT_BYTES)


def _silu(x):
    return x * (1.0 / (1.0 + jnp.exp(-x)))


def _ada_kernel(cc_ref, a_ref, b_ref, bias_ref, o_ref):
    s = _silu(cc_ref[...])
    t = jnp.dot(s, a_ref[0], precision=HIGHEST, preferred_element_type=F32)
    o_ref[0] = jnp.dot(t, b_ref[0], precision=HIGHEST, preferred_element_type=F32) + bias_ref[0]


def _ada_mods(cc, ada_a, ada_b, ada_bias):
    depth, d, r = ada_a.shape
    n = ada_b.shape[2]
    tn = _pick(n, 2048, LANES)
    return pl.pallas_call(
        _ada_kernel,
        grid=(depth, n // tn),
        in_specs=[
            pl.BlockSpec((SUBLANES, d), lambda l, j: (0, 0)),
            pl.BlockSpec((1, d, r), lambda l, j: (l, 0, 0)),
            pl.BlockSpec((1, r, tn), lambda l, j: (l, 0, j)),
            pl.BlockSpec((1, 1, tn), lambda l, j: (l, 0, j)),
        ],
        out_specs=pl.BlockSpec((1, SUBLANES, tn), lambda l, j: (l, 0, j)),
        out_shape=jax.ShapeDtypeStruct((depth, SUBLANES, n), F32),
        compiler_params=_params("parallel", "parallel"),
        name="ada_mods",
    )(cc, ada_a, ada_b, ada_bias.reshape(depth, 1, n))


def _norm_mod_kernel(x_ref, g_ref, sh_ref, sc_ref, o_ref):
    x = x_ref[...]
    ms = jnp.mean(x * x, axis=-1, keepdims=True)
    y = x * lax.rsqrt(ms + NORM_EPS) * g_ref[...]
    o_ref[...] = (y * (1.0 + sc_ref[...]) + sh_ref[...]).astype(o_ref.dtype)


def _norm_mod(x, gain, shift, scale):
    m, d = x.shape
    tr = _pick(m, 256, SUBLANES)
    vec = pl.BlockSpec((1, d), lambda i: (0, 0))
    return pl.pallas_call(
        _norm_mod_kernel,
        grid=(m // tr,),
        in_specs=[pl.BlockSpec((tr, d), lambda i: (i, 0)), vec, vec, vec],
        out_specs=pl.BlockSpec((tr, d), lambda i: (i, 0)),
        out_shape=jax.ShapeDtypeStruct((m, d), BF16),
        compiler_params=_params("parallel"),
        name="norm_mod",
    )(x, gain.reshape(1, d), shift.reshape(1, d), scale.reshape(1, d))


def _head_norm_rope(y, gain, cos, sin, out_scale):
    outs = []
    for h in range(y.shape[1] // HEAD_DIM):
        t = y[:, h * HEAD_DIM:(h + 1) * HEAD_DIM]
        ms = jnp.mean(t * t, axis=-1, keepdims=True)
        t = t * lax.rsqrt(ms + NORM_EPS) * gain
        if cos is not None:
            t = t * cos + pltpu.roll(t, HEAD_DIM // 2, axis=1) * sin
        if out_scale != 1.0:
            t = t * out_scale
        outs.append(t)
    return outs[0] if len(outs) == 1 else jnp.concatenate(outs, axis=1)


def _mm_kernel(*refs, mode, nk, out_scale):
    n_w = 2 if mode == "swiglu" else 1
    a_ref, w_refs = refs[0], refs[1:1 + n_w]
    n_extra = {"bias": 1, "plain": 0, "swiglu": 0, "resid": 2, "headnorm": 1, "headrope": 3}[mode]
    extra = refs[1 + n_w:1 + n_w + n_extra]
    o_ref = refs[1 + n_w + n_extra]
    accs = refs[2 + n_w + n_extra:]
    k = pl.program_id(2)

    def epilogue(ys, rows=slice(None)):
        y = ys[0]
        if mode == "bias":
            y = y + extra[0][...]
        elif mode == "swiglu":
            y = _silu(y) * ys[1]
        elif mode == "resid":
            y = extra[0][rows, :] + extra[1][...] * y
        elif mode == "headnorm":
            y = _head_norm_rope(y, extra[0][...], None, None, out_scale)
        elif mode == "headrope":
            y = _head_norm_rope(y, extra[0][...], extra[1][rows, :], extra[2][rows, :], out_scale)
        o_ref[rows, :] = y.astype(o_ref.dtype)

    if nk == 1:
        tm = a_ref.shape[0]
        rc = MM_ROW_CHUNK if tm % MM_ROW_CHUNK == 0 else tm
        pending = None
        for r in range(0, tm, rc):
            rows = slice(r, r + rc)
            a = a_ref[rows, :]
            parts = [jnp.dot(a, w[...], preferred_element_type=F32) for w in w_refs]
            if pending is not None:
                epilogue(*pending)
            pending = (parts, rows)
        epilogue(*pending)
        return

    a = a_ref[...]
    parts = [jnp.dot(a, w[...], preferred_element_type=F32) for w in w_refs]

    @pl.when(k == 0)
    def _():
        for acc, p in zip(accs, parts):
            acc[...] = p

    @pl.when(jnp.logical_and(k > 0, k < nk - 1))
    def _():
        for acc, p in zip(accs, parts):
            acc[...] += p

    @pl.when(k == nk - 1)
    def _():
        epilogue([acc[...] + p for acc, p in zip(accs, parts)])


def _matmul(a, ws, mode, out_dtype, *, extra=(), tm=1024, tn=512, tk=None, out_scale=1.0):
    m, kdim = a.shape
    n = ws[0].shape[1]
    tm = _pick(m, tm, BF16_ROWS)
    tn = _pick(n, tn, LANES)
    tk = kdim if tk is None else _pick(kdim, tk, LANES)
    nk = kdim // tk
    grid = (m // tm, n // tn, nk)
    in_specs = [pl.BlockSpec((tm, tk), lambda i, j, k: (i, k))]
    in_specs += [pl.BlockSpec((tk, tn), lambda i, j, k: (k, j)) for _ in ws]
    row = pl.BlockSpec((1, tn), lambda i, j, k: (0, j))
    head = pl.BlockSpec((1, HEAD_DIM), lambda i, j, k: (0, 0))
    ops = []
    if mode == "bias":
        in_specs += [row]
        ops = [extra[0].reshape(1, n)]
    elif mode == "resid":
        in_specs += [pl.BlockSpec((tm, tn), lambda i, j, k: (i, j)), row]
        ops = [extra[0], extra[1].reshape(1, n)]
    elif mode == "headnorm":
        in_specs += [head]
        ops = [extra[0].reshape(1, HEAD_DIM)]
    elif mode == "headrope":
        tab = pl.BlockSpec((tm, HEAD_DIM), lambda i, j, k: (i, 0))
        in_specs += [head, tab, tab]
        ops = [extra[0].reshape(1, HEAD_DIM), extra[1], extra[2]]
    n_acc = len(ws) if nk > 1 else 0
    return pl.pallas_call(
        functools.partial(_mm_kernel, mode=mode, nk=nk, out_scale=out_scale),
        grid=grid,
        in_specs=in_specs,
        out_specs=pl.BlockSpec((tm, tn), lambda i, j, k: (i, j)),
        out_shape=jax.ShapeDtypeStruct((m, n), out_dtype),
        scratch_shapes=[pltpu.VMEM((tm, tn), F32) for _ in range(n_acc)],
        compiler_params=_params("parallel", "parallel", "arbitrary"),
        name="mm_" + mode,
    )(a, *ws, *ops)


def _attn_kernel(*refs, kind, n_kv, lam_init):
    if kind == "gqa":
        q_ref, k_ref, vt_ref, o_ref, m_sc, acc_sc = refs
        n_q = GQA_GROUP
    else:
        q_ref, k_ref, vt_ref, lq1, lk1, lq2, lk2, sg_ref, o_ref, m_sc, acc_sc = refs
        n_q = 2
    kv = pl.program_id(2)
    dv = vt_ref.shape[0] - BF16_ROWS

    @pl.when(kv == 0)
    def _():
        m_sc[...] = jnp.full(m_sc.shape, -jnp.inf, F32)
        acc_sc[...] = jnp.zeros(acc_sc.shape, F32)

    vt = vt_ref[...]
    tq = q_ref.shape[0]
    tc = min(tq, ATTN_COL_TILE)
    units = [(h, c) for h in range(n_q) for c in range(tq // tc)]

    def scores_t(u):
        h, c = units[u]
        q = q_ref[c * tc:(c + 1) * tc, h * HEAD_DIM:(h + 1) * HEAD_DIM]
        kh = 0 if kind == "gqa" else h
        k = k_ref[:, kh * HEAD_DIM:(kh + 1) * HEAD_DIM]
        return lax.dot_general(k, q, (((1,), (1,)), ((), ())), preferred_element_type=F32)

    def softmax_t(u, st):
        h, c = units[u]
        m_prev = m_sc[h, :, c * tc:(c + 1) * tc]
        m_new = jnp.maximum(m_prev, jnp.max(st, axis=0, keepdims=True))
        m_sc[h, :, c * tc:(c + 1) * tc] = m_new
        return jnp.exp2(st - m_new).astype(vt.dtype), jnp.exp2(m_prev - m_new)

    def accumulate(u, p, alpha):
        h, c = units[u]
        acc = acc_sc[h, :, c * tc:(c + 1) * tc]
        acc_sc[h, :, c * tc:(c + 1) * tc] = alpha * acc + jnp.dot(vt, p, preferred_element_type=F32)

    st, pa = {}, {}
    lag_b, lag_c = ATTN_SOFTMAX_LAG[kind], ATTN_SOFTMAX_LAG[kind] + ATTN_VALUE_LAG[kind]
    for i in range(len(units) + lag_c):
        if i < len(units):
            st[i] = scores_t(i)
        if 0 <= i - lag_b < len(units):
            pa[i - lag_b] = softmax_t(i - lag_b, st.pop(i - lag_b))
        if 0 <= i - lag_c < len(units):
            accumulate(i - lag_c, *pa.pop(i - lag_c))

    @pl.when(kv == n_kv - 1)
    def _():
        def normalised(h):
            acc = acc_sc[h]
            return acc[:dv] / acc[dv:dv + 1]

        if kind == "gqa":
            for h in range(n_q):
                o_ref[:, h * HEAD_DIM:(h + 1) * HEAD_DIM] = normalised(h).T.astype(o_ref.dtype)
        else:
            lam = (jnp.exp(jnp.sum(lq1[...] * lk1[...], axis=-1, keepdims=True))
                   - jnp.exp(jnp.sum(lq2[...] * lk2[...], axis=-1, keepdims=True)) + lam_init)
            o = (normalised(0) - lam * normalised(1)).T
            ms = jnp.mean(o * o, axis=-1, keepdims=True)
            o = o * lax.rsqrt(ms + NORM_EPS) * sg_ref[...] * (1.0 - lam_init)
            o_ref[...] = o.astype(o_ref.dtype)


def _attention(q, k, v, kind, *, lam_params=None, sub_gain=None, lam_init=0.0):
    lq, d = q.shape
    t = k.shape[0]
    if kind == "gqa":
        n_heads = d // (GQA_GROUP * HEAD_DIM)
        qw, kw, vw, n_q = GQA_GROUP * HEAD_DIM, HEAD_DIM, HEAD_DIM, GQA_GROUP
    else:
        n_heads = d // (2 * HEAD_DIM)
        qw, kw, vw, n_q = 2 * HEAD_DIM, 2 * HEAD_DIM, 2 * HEAD_DIM, 2
    tq = _pick(lq, 1024, ATTN_COL_TILE)
    tk = _pick(t, 1280, LANES)
    n_kv = t // tk
    vt = v.T.reshape(v.shape[1] // vw, vw, t)
    ones = jnp.zeros((vt.shape[0], BF16_ROWS, t), v.dtype).at[:, 0, :].set(1.0)
    vt = jnp.concatenate([vt, ones], axis=1)
    vr = vw + BF16_ROWS
    in_specs = [
        pl.BlockSpec((tq, qw), lambda h, i, j: (i, h)),
        pl.BlockSpec((tk, kw), lambda h, i, j: (j, h)),
        pl.BlockSpec((None, vr, tk), lambda h, i, j: (h, 0, j)),
    ]
    ops = [q, k, vt]
    if kind == "diff":
        vec = pl.BlockSpec((1, HEAD_DIM), lambda h, i, j: (0, 0))
        in_specs += [vec, vec, vec, vec, pl.BlockSpec((1, 2 * HEAD_DIM), lambda h, i, j: (0, 0))]
        ops += [p.reshape(1, HEAD_DIM) for p in lam_params] + [sub_gain.reshape(1, 2 * HEAD_DIM)]
    return pl.pallas_call(
        functools.partial(_attn_kernel, kind=kind, n_kv=n_kv, lam_init=lam_init),
        grid=(n_heads, lq // tq, n_kv),
        in_specs=in_specs,
        out_specs=pl.BlockSpec((tq, qw), lambda h, i, j: (i, h)),
        out_shape=jax.ShapeDtypeStruct((lq, d), BF16),
        scratch_shapes=[
            pltpu.VMEM((n_q, 1, tq), F32),
            pltpu.VMEM((n_q, vr, tq), F32),
        ],
        compiler_params=_params("parallel", "parallel", "arbitrary"),
        name="attn_" + kind,
    )(*ops)


def _sconv_kernel(z_ref, zp_ref, zn_ref, w_ref, b_ref, o_ref, *, n_row_blocks):
    i = pl.program_id(1)
    z = z_ref[...]
    tr = z.shape[0]
    prev_row = jnp.where(i == 0, 0.0, zp_ref[SUBLANES - 1:SUBLANES, :])
    next_row = jnp.where(i == n_row_blocks - 1, 0.0, zn_ref[0:1, :])
    row = lax.broadcasted_iota(jnp.int32, z.shape, 0)
    z_before = jnp.where(row == 0, prev_row, pltpu.roll(z, 1, axis=0))
    z_after = jnp.where(row == tr - 1, next_row, pltpu.roll(z, tr - 1, axis=0))
    w = w_ref[...]
    o_ref[0] = b_ref[...] + z_before * w[0:1] + z * w[1:2] + z_after * w[2:3]


def _short_conv(z, conv_w, conv_b, n_split):
    l, c = z.shape
    d = c // n_split
    tr = _pick(l, 512, SUBLANES)
    tc = _pick(d, 1024, LANES)
    ncb = d // tc
    nrb = l // tr
    r8 = tr // SUBLANES
    last8 = l // SUBLANES - 1
    return pl.pallas_call(
        functools.partial(_sconv_kernel, n_row_blocks=nrb),
        grid=(n_split, nrb, ncb),
        in_specs=[
            pl.BlockSpec((tr, tc), lambda s, i, j: (i, s * ncb + j)),
            pl.BlockSpec((SUBLANES, tc), lambda s, i, j: (jnp.maximum(i * r8 - 1, 0), s * ncb + j)),
            pl.BlockSpec((SUBLANES, tc), lambda s, i, j: (jnp.minimum((i + 1) * r8, last8), s * ncb + j)),
            pl.BlockSpec((conv_w.shape[0], tc), lambda s, i, j: (0, s * ncb + j)),
            pl.BlockSpec((1, tc), lambda s, i, j: (0, s * ncb + j)),
        ],
        out_specs=pl.BlockSpec((1, tr, tc), lambda s, i, j: (s, i, j)),
        out_shape=jax.ShapeDtypeStruct((n_split, l, d), F32),
        compiler_params=_params("parallel", "parallel", "parallel"),
        name="short_conv",
    )(z, z, z, conv_w, conv_b.reshape(1, c))


def _filter_mlp_kernel(f_ref, w1_ref, b1_ref, w2_ref, b2_ref, fr_ref, o_ref):
    g = jnp.dot(f_ref[...], w1_ref[...], precision=HIGHEST, preferred_element_type=F32)
    g = jnp.sin(fr_ref[0:1] * (g + b1_ref[...]))
    g = jnp.dot(g, w2_ref[...], precision=HIGHEST, preferred_element_type=F32)
    o_ref[...] = jnp.sin(fr_ref[1:2] * (g + b2_ref[...]))


def _filter_taps_fft1_kernel(g_ref, w3f_ref, w3b_ref, dl_ref, f_ref, tc_ref, ts_ref,
                             are_ref, aim_ref, asum_ref, *, n_tok, n2_size, k1p):
    n2 = pl.program_id(1)
    n1_size = g_ref.shape[0]
    half = n1_size // 2
    g = g_ref[...]
    taps = jnp.concatenate([
        jnp.dot(g[:half], w3f_ref[...], precision=HIGHEST, preferred_element_type=F32),
        jnp.dot(g[half:], w3b_ref[...], precision=HIGHEST, preferred_element_type=F32)], axis=0)
    n = lax.broadcasted_iota(jnp.int32, (n1_size, 1), 0) * n2_size + n2
    tau = jnp.where(n < n_tok, n, 2 * n_tok - n)
    t = tau.astype(F32) * (1.0 / (n_tok - 1))
    taps = taps * jnp.exp(-t * dl_ref[...])
    taps = jnp.where(n == n_tok, 0.0, taps)

    @pl.when(n2 == 0)
    def _():
        asum_ref[...] = jnp.zeros(asum_ref.shape, F32)

    asum_ref[...] += jnp.sum(jnp.abs(taps), axis=0, keepdims=True)

    a = jnp.dot(f_ref[...], _split3(taps), preferred_element_type=F32)
    re, im = a[:k1p], a[k1p:]
    c = _table_column(tc_ref, n2)
    s = _table_column(ts_ref, n2)
    are_ref[...] = re * c + im * s
    aim_ref[...] = im * c - re * s


def _hyena_filter_stage1(n_tok, d, consts, f_w1, f_b1, f_w2, f_b2, f_w3, f_freq):
    width = f_w1.shape[1]
    t = jnp.linspace(0.0, 1.0, n_tok, dtype=F32)[:, None]
    bands = (HY_EMB_DIM - 1) // 2
    w = (2.0 * math.pi / n_tok) * jnp.arange(n_tok, dtype=F32)[:, None]
    f = jnp.linspace(1e-4, bands - 1, bands, dtype=F32)[None, :]
    feats = jnp.concatenate([t, jnp.cos(f * w), -jnp.sin(f * w)], axis=-1)
    feats2 = jnp.concatenate([feats, feats[:1], feats[:0:-1]], axis=0)
    feats2 = jnp.pad(feats2, ((0, 0), (0, LANES - HY_EMB_DIM)))
    pw = LANES - width
    w1 = jnp.pad(f_w1, ((0, LANES - HY_EMB_DIM), (0, pw)))
    w2 = jnp.pad(f_w2, ((0, pw), (0, pw)))
    b1 = jnp.pad(f_b1, (0, pw)).reshape(1, LANES)
    b2 = jnp.pad(f_b2, (0, pw)).reshape(1, LANES)
    freq = jnp.pad(f_freq, ((0, 0), (0, pw)))
    n2l = 2 * n_tok
    tr = _pick(n_tok, 512, SUBLANES)
    full = lambda shape: pl.BlockSpec(shape, lambda i: (0,) * len(shape))
    g = pl.pallas_call(
        _filter_mlp_kernel,
        grid=(n2l // tr,),
        in_specs=[pl.BlockSpec((tr, LANES), lambda i: (i, 0)), full((LANES, LANES)), full((1, LANES)),
                  full((LANES, LANES)), full((1, LANES)), full((2, LANES))],
        out_specs=pl.BlockSpec((tr, LANES), lambda i: (i, 0)),
        out_shape=jax.ShapeDtypeStruct((n2l, LANES), F32),
        compiler_params=_params("parallel"),
        name="hyena_filter_mlp",
    )(feats2, w1, b1, w2, b2, freq)

    c = HY_ORDER * d
    w3 = jnp.pad(f_w3, ((0, pw), (0, 0)))
    max_decay = math.log(HY_DECAY_TARGET) / HY_FAST_DECAY_PCT
    min_decay = math.log(HY_DECAY_TARGET) / HY_SLOW_DECAY_PCT
    deltas = jnp.abs(jnp.linspace(min_decay, max_decay, d, dtype=F32))
    deltas = jnp.tile(deltas, HY_ORDER).reshape(1, c)
    _, n1, n2, _, k1p = _dft_plan(n_tok)
    g_by_n2 = g.reshape(n1, n2, LANES).transpose(1, 0, 2)
    tcol = _pick(c, 2048, LANES)
    ncb = c // tcol
    tab = pl.BlockSpec((k1p, n2), lambda j, i: (0, 0))
    out = pl.BlockSpec((k1p, tcol), lambda j, i: (0, i * ncb + j))
    return pl.pallas_call(
        functools.partial(_filter_taps_fft1_kernel, n_tok=n_tok, n2_size=n2, k1p=k1p),
        grid=(ncb, n2),
        in_specs=[
            pl.BlockSpec((None, n1, LANES), lambda j, i: (i, 0, 0)),
            pl.BlockSpec((LANES, tcol), lambda j, i: (0, j)),
            pl.BlockSpec((LANES, tcol), lambda j, i: (0, ncb + j)),
            pl.BlockSpec((1, tcol), lambda j, i: (0, j)),
            pl.BlockSpec((2 * k1p, 3 * n1), lambda j, i: (0, 0)), tab, tab,
        ],
        out_specs=[out, out, pl.BlockSpec((1, tcol), lambda j, i: (0, j))],
        out_shape=[jax.ShapeDtypeStruct((k1p, n2 * c), F32)] * 2 + [jax.ShapeDtypeStruct((1, c), F32)],
        compiler_params=_params("parallel", "arbitrary"),
        name="hyena_filter_fft1",
    )(g_by_n2, w3, w3, deltas, consts["f1_full"], consts["tw_cos"], consts["tw_sin"])


def _split3(x):
    hi = x.astype(BF16)
    lo = (x - hi.astype(F32)).astype(BF16)
    return jnp.concatenate([hi, lo, hi], axis=0)


def _mat3(m64):
    m = jnp.asarray(m64, F32)
    hi = m.astype(BF16)
    lo = (m - hi.astype(F32)).astype(BF16)
    return jnp.concatenate([hi, hi, lo], axis=1)


def _dft_plan(n_tok):
    n = 2 * n_tok
    n1 = 1 << ((n.bit_length() - 1 + 1) // 2)
    n2 = n // n1
    assert n1 * n2 == n and n2 >= SUBLANES, (n_tok, n1, n2)
    k1 = n1 // 2 + 1
    k1p = -(-k1 // BF16_ROWS) * BF16_ROWS
    return n, n1, n2, k1, k1p


def _dft_constants(n_tok):
    n, n1, n2, k1, k1p = _dft_plan(n_tok)
    two_pi = 2.0 * np.pi
    kk = np.arange(k1p)[:, None]
    valid = (kk < k1).astype(np.float64)
    ang = two_pi * ((kk * np.arange(n1)[None, :]) % n1) / n1
    f1 = np.concatenate([np.cos(ang) * valid, -np.sin(ang) * valid], axis=0)
    a2 = two_pi * ((np.arange(n2)[:, None] * np.arange(n2)[None, :]) % n2) / n2
    c2, s2 = np.cos(a2), np.sin(a2)
    f2 = np.block([[c2, s2], [-s2, c2]])
    g2 = np.block([[c2, -s2], [s2, c2]])
    wk = np.where((np.arange(k1p) == 0) | (np.arange(k1p) == n1 // 2), 1.0, 2.0) * (np.arange(k1p) < k1)
    a3 = two_pi * ((np.arange(n1 // 2)[:, None] * np.arange(k1p)[None, :]) % n1) / n1
    g3 = np.concatenate([np.cos(a3) * wk[None, :], -np.sin(a3) * wk[None, :]], axis=1) / n
    at = two_pi * (kk * np.arange(n2)[None, :]) / n
    return dict(
        f1_full=_mat3(f1), f1_half=_mat3(f1[:, :n1 // 2]), f2=_mat3(f2), g2=_mat3(g2), g3=_mat3(g3),
        tw_cos=jnp.asarray(np.cos(at), F32), tw_sin=jnp.asarray(np.sin(at), F32),
        tw_cos_t=jnp.asarray(np.cos(at).T, F32), tw_sin_t=jnp.asarray(np.sin(at).T, F32),
    )


def _table_column(tab_ref, idx):
    lane = lax.broadcasted_iota(jnp.int32, tab_ref.shape, 1)
    return jnp.sum(jnp.where(lane == idx, tab_ref[...], 0.0), axis=1, keepdims=True)


def _fft1_kernel(x_ref, f_ref, tc_ref, ts_ref, are_ref, aim_ref, *, k1p, blocks_per_n2):
    n2 = pl.program_id(0) // blocks_per_n2
    a = jnp.dot(f_ref[...], _split3(x_ref[...]), preferred_element_type=F32)
    re, im = a[:k1p], a[k1p:]
    c = _table_column(tc_ref, n2)
    s = _table_column(ts_ref, n2)
    are_ref[...] = re * c + im * s
    aim_ref[...] = im * c - re * s


def _fft_stage1(x2d, f1, consts, n_tok, n_ch):
    _, n1, n2, _, k1p = _dft_plan(n_tok)
    n1c, cols = x2d.shape
    tcol = _pick(n_ch, 2048, LANES)
    tab = pl.BlockSpec((k1p, n2), lambda j: (0, 0))
    out = pl.BlockSpec((k1p, tcol), lambda j: (0, j))
    return pl.pallas_call(
        functools.partial(_fft1_kernel, k1p=k1p, blocks_per_n2=n_ch // tcol),
        grid=(cols // tcol,),
        in_specs=[pl.BlockSpec((n1c, tcol), lambda j: (0, j)),
                  pl.BlockSpec((2 * k1p, 3 * n1c), lambda j: (0, 0)), tab, tab],
        out_specs=[out, out],
        out_shape=[jax.ShapeDtypeStruct((k1p, cols), F32)] * 2,
        compiler_params=_params("parallel"),
        name="hyena_fft1",
    )(x2d, f1, consts["tw_cos"], consts["tw_sin"])


def _fft2_spec_kernel(are_ref, aim_ref, f_ref, asum_ref, sre_ref, sim_ref, *, tk1, n2):
    inv = 1.0 / (asum_ref[...] + NORM_EPS)

    def body(r, carry):
        a = jnp.concatenate([are_ref[r], aim_ref[r]], axis=0)
        x = jnp.dot(f_ref[...], _split3(a), preferred_element_type=F32)
        sre_ref[r] = x[:n2] * inv
        sim_ref[r] = x[n2:] * inv
        return carry

    lax.fori_loop(0, tk1, body, 0, unroll=2)


def _fft2_conv_kernel(are_ref, aim_ref, sre_ref, sim_ref, f_ref, g_ref, tc_ref, ts_ref,
                      bre_ref, bim_ref, *, tk1, n2):
    i = pl.program_id(0)

    def body(r, carry):
        a = jnp.concatenate([are_ref[r], aim_ref[r]], axis=0)
        x = jnp.dot(f_ref[...], _split3(a), preferred_element_type=F32)
        xr, xi = x[:n2], x[n2:]
        sr, si = sre_ref[r], sim_ref[r]
        y = jnp.concatenate([xr * sr - xi * si, xr * si + xi * sr], axis=0)
        b = jnp.dot(g_ref[...], _split3(y), preferred_element_type=F32)
        br, bi = b[:n2], b[n2:]
        c = _table_column(tc_ref, i * tk1 + r)
        s = _table_column(ts_ref, i * tk1 + r)
        bre_ref[r] = br * c - bi * s
        bim_ref[r] = bi * c + br * s
        return carry

    lax.fori_loop(0, tk1, body, 0, unroll=2)


def _fft_stage2(are, aim, consts, n_tok, n_ch, *, spec=None, asum=None, spec_offset=0):
    _, n1, n2, _, k1p = _dft_plan(n_tok)
    tk1 = SUBLANES
    td = _pick(n_ch, 512, LANES)
    blk = pl.BlockSpec((tk1, n2, td), lambda i, j: (i, 0, j))
    mat = pl.BlockSpec((2 * n2, 6 * n2), lambda i, j: (0, 0))
    grid = (k1p // tk1, n_ch // td)
    out_shape = [jax.ShapeDtypeStruct((k1p, n2, n_ch), F32)] * 2
    if spec is None:
        return pl.pallas_call(
            functools.partial(_fft2_spec_kernel, tk1=tk1, n2=n2),
            grid=grid,
            in_specs=[blk, blk, mat, pl.BlockSpec((1, td), lambda i, j: (0, j))],
            out_specs=[blk, blk],
            out_shape=out_shape,
            compiler_params=_params("parallel", "parallel"),
            name="hyena_fft2_spec",
        )(are, aim, consts["f2"], asum)
    off = spec_offset // td
    sblk = pl.BlockSpec((tk1, n2, td), lambda i, j: (i, 0, j + off))
    tab = pl.BlockSpec((n2, k1p), lambda i, j: (0, 0))
    return pl.pallas_call(
        functools.partial(_fft2_conv_kernel, tk1=tk1, n2=n2),
        grid=grid,
        in_specs=[blk, blk, sblk, sblk, mat, mat, tab, tab],
        out_specs=[blk, blk],
        out_shape=out_shape,
        compiler_params=_params("parallel", "parallel"),
        name="hyena_fft2_conv",
    )(are, aim, spec[0], spec[1], consts["f2"], consts["g2"], consts["tw_cos_t"], consts["tw_sin_t"])


def _fft3_kernel(bre_ref, bim_ref, g_ref, y_ref, gate_ref, bias_ref, o_ref):
    b = jnp.concatenate([bre_ref[...], bim_ref[...]], axis=0)
    conv = jnp.dot(g_ref[...], _split3(b), preferred_element_type=F32)
    o_ref[...] = (gate_ref[...] * (conv + y_ref[...] * bias_ref[...])).astype(o_ref.dtype)


def _fft_stage3(bre, bim, consts, n_tok, n_ch, y2d, gate2d, bias, out_dtype):
    _, n1, n2, _, k1p = _dft_plan(n_tok)
    cols = n2 * n_ch
    tcol = _pick(n_ch, 2048, LANES)
    per = n_ch // tcol
    inb = pl.BlockSpec((k1p, tcol), lambda j: (0, j))
    sig = pl.BlockSpec((n1 // 2, tcol), lambda j: (0, j))
    return pl.pallas_call(
        _fft3_kernel,
        grid=(cols // tcol,),
        in_specs=[inb, inb, pl.BlockSpec((n1 // 2, 6 * k1p), lambda j: (0, 0)), sig, sig,
                  pl.BlockSpec((1, tcol), lambda j: (0, j % per))],
        out_specs=sig,
        out_shape=jax.ShapeDtypeStruct((n1 // 2, cols), out_dtype),
        compiler_params=_params("parallel"),
        name="hyena_fft3",
    )(bre, bim, consts["g3"], y2d, gate2d, bias.reshape(1, n_ch))


def _hyena_mixer_core(h, w_in, b_in, conv_w, conv_b, f_w1, f_b1, f_w2, f_b2, f_w3, f_freq, f_bias):
    n_tok, d = h.shape
    _, n1, n2, _, k1p = _dft_plan(n_tok)
    consts = _dft_constants(n_tok)
    z = _matmul(h, (w_in,), "bias", F32, extra=(b_in,))
    zs = _short_conv(z, conv_w, conv_b, HY_ORDER + 1)

    c = HY_ORDER * d
    tre, tim, asum = _hyena_filter_stage1(n_tok, d, consts, f_w1, f_b1, f_w2, f_b2, f_w3, f_freq)
    spec = _fft_stage2(tre.reshape(k1p, n2, c), tim.reshape(k1p, n2, c), consts, n_tok, c, asum=asum)

    y = zs[0].reshape(n1 // 2, n2 * d)
    for n in range(HY_ORDER):
        are, aim = _fft_stage1(y, consts["f1_half"], consts, n_tok, d)
        bre, bim = _fft_stage2(are.reshape(k1p, n2, d), aim.reshape(k1p, n2, d), consts, n_tok, d,
                               spec=spec, spec_offset=n * d)
        last = n == HY_ORDER - 1
        y = _fft_stage3(bre.reshape(k1p, n2 * d), bim.reshape(k1p, n2 * d), consts, n_tok, d, y,
                        zs[n + 1].reshape(n1 // 2, n2 * d), f_bias[n], BF16 if last else F32)
    return y.reshape(n_tok, d)


def _axial_rope_tables(n_tok):
    rows = n_tok // GRID_W
    row = jnp.repeat(jnp.arange(rows, dtype=F32), GRID_W)
    col = (jnp.arange(rows * GRID_W) % GRID_W).astype(F32)
    n_pair = HEAD_DIM // 4
    inv_freq = ROPE_THETA ** (-jnp.arange(n_pair, dtype=F32) / n_pair)
    ang = jnp.concatenate([row[:, None] * inv_freq, col[:, None] * inv_freq], axis=-1)
    cos, sin = jnp.cos(ang), jnp.sin(ang)
    return jnp.concatenate([cos, cos], axis=-1), jnp.concatenate([-sin, sin], axis=-1)


def _split_half_cols(w):
    k, n = w.shape
    return w.reshape(k, n // HEAD_DIM, HEAD_DIM // 2, 2).transpose(0, 1, 3, 2).reshape(k, n)


def _split_half_vec(g):
    return g.reshape(HEAD_DIM // 2, 2).T.reshape(HEAD_DIM)


def kernel(x, c, ctx, c_ctx, ada_a, ada_b, ada_bias, norm1_g, norm2_g, ffn_w_gate, ffn_w_up, ffn_w_down,
           hy_w_in, hy_b_in, hy_conv_w, hy_conv_b, hy_f_w1, hy_f_b1, hy_f_w2, hy_f_b2, hy_f_w3, hy_f_freq,
           hy_f_bias, hy_w_out, gqa_w_q, gqa_w_k, gqa_w_v, gqa_q_norm, gqa_k_norm, gqa_w_o,
           diff_w_q, diff_w_k, diff_w_v, diff_q_norm, diff_k_norm, diff_lambda_q1, diff_lambda_k1,
           diff_lambda_q2, diff_lambda_k2, diff_sub_norm, diff_w_o):
    assert x.shape[0] == 1 and ctx.shape[0] == 1
    depth = ada_a.shape[0]
    n_tok, d = x.shape[1], x.shape[2]
    xs = x[0]
    cs = ctx[0]
    cos, sin = _axial_rope_tables(n_tok)

    cc = jnp.concatenate([c.reshape(1, d), c_ctx.reshape(1, d), jnp.zeros((SUBLANES - 2, d), F32)], axis=0)
    mods = _ada_mods(cc, ada_a, ada_b, ada_bias)

    kinds = [i % N_MIXERS for i in range(depth)]
    for i in range(depth):
        kind = kinds[i]
        j = i // N_MIXERS
        reads_ctx = kind != MIX_HYENA
        update_ctx = any(kk != MIX_HYENA for kk in kinds[i + 1:])
        sh1, sc1, g1, sh2, sc2, g2 = [mods[i, 0, m * d:(m + 1) * d] for m in range(N_MOD)]
        csh1, csc1, cg1, csh2, csc2, cg2 = [mods[i, 1, m * d:(m + 1) * d] for m in range(N_MOD)]
        h = _norm_mod(xs, norm1_g[i], sh1, sc1)
        hc = _norm_mod(cs, norm1_g[i], csh1, csc1) if (reads_ctx or update_ctx) else None
        yc = None
        if kind == MIX_HYENA:
            w_in = hy_w_in[j].astype(BF16)
            w_out = hy_w_out[j].astype(BF16)
            args = (w_in, hy_b_in[j], hy_conv_w[j], hy_conv_b[j], hy_f_w1[j], hy_f_b1[j], hy_f_w2[j],
                    hy_f_b2[j], hy_f_w3[j], hy_f_freq[j], hy_f_bias[j])
            y = _hyena_mixer_core(h, *args)
            xs = _matmul(y, (w_out,), "resid", F32, extra=(xs, g1))
            if update_ctx:
                yc = _hyena_mixer_core(hc, *args)
                cs = _matmul(yc, (w_out,), "resid", F32, extra=(cs, cg1))
        elif kind == MIX_GQA:
            w_q = _split_half_cols(gqa_w_q[j]).astype(BF16)
            w_k = _split_half_cols(gqa_w_k[j]).astype(BF16)
            w_v = gqa_w_v[j].astype(BF16)
            w_o = gqa_w_o[j].astype(BF16)
            qg = _split_half_vec(gqa_q_norm[j])
            kg = _split_half_vec(gqa_k_norm[j])
            q = _matmul(h, (w_q,), "headrope", BF16, extra=(qg, cos, sin), out_scale=ATTN_Q_SCALE)
            k = _matmul(h, (w_k,), "headrope", BF16, extra=(kg, cos, sin))
            v = _matmul(h, (w_v,), "plain", BF16)
            kc = _matmul(hc, (w_k,), "headnorm", BF16, extra=(kg,))
            vc = _matmul(hc, (w_v,), "plain", BF16)
            o = _attention(q, jnp.concatenate([k, kc], axis=0), jnp.concatenate([v, vc], axis=0), "gqa")
            xs = _matmul(o, (w_o,), "resid", F32, extra=(xs, g1))
            if update_ctx:
                qc = _matmul(hc, (w_q,), "headnorm", BF16, extra=(qg,), out_scale=ATTN_Q_SCALE)
                oc = _attention(qc, kc, vc, "gqa")
                cs = _matmul(oc, (w_o,), "resid", F32, extra=(cs, cg1))
        else:
            w_q = _split_half_cols(diff_w_q[j]).astype(BF16)
            w_k = _split_half_cols(diff_w_k[j]).astype(BF16)
            w_v = diff_w_v[j].astype(BF16)
            w_o = diff_w_o[j].astype(BF16)
            qg = _split_half_vec(diff_q_norm[j])
            kg = _split_half_vec(diff_k_norm[j])
            lam_init = 0.8 - 0.6 * math.exp(-0.3 * i)
            lam_params = (diff_lambda_q1[j], diff_lambda_k1[j], diff_lambda_q2[j], diff_lambda_k2[j])
            attend = functools.partial(_attention, kind="diff", lam_params=lam_params,
                                       sub_gain=diff_sub_norm[j], lam_init=lam_init)
            q = _matmul(h, (w_q,), "headrope", BF16, extra=(qg, cos, sin), out_scale=ATTN_Q_SCALE)
            k = _matmul(h, (w_k,), "headrope", BF16, extra=(kg, cos, sin))
            v = _matmul(h, (w_v,), "plain", BF16)
            kc = _matmul(hc, (w_k,), "headnorm", BF16, extra=(kg,))
            vc = _matmul(hc, (w_v,), "plain", BF16)
            o = attend(q, jnp.concatenate([k, kc], axis=0), jnp.concatenate([v, vc], axis=0))
            xs = _matmul(o, (w_o,), "resid", F32, extra=(xs, g1))
            if update_ctx:
                qc = _matmul(hc, (w_q,), "headnorm", BF16, extra=(qg,), out_scale=ATTN_Q_SCALE)
                oc = attend(qc, kc, vc)
                cs = _matmul(oc, (w_o,), "resid", F32, extra=(cs, cg1))

        w_gate = ffn_w_gate[i].astype(BF16)
        w_up = ffn_w_up[i].astype(BF16)
        w_down = ffn_w_down[i].astype(BF16)
        h2 = _norm_mod(xs, norm2_g[i], sh2, sc2)
        act = _matmul(h2, (w_gate, w_up), "swiglu", BF16, tn=256)
        xs = _matmul(act, (w_down,), "resid", F32, extra=(xs, g2), tm=512, tn=256)
        if update_ctx:
            hc2 = _norm_mod(cs, norm2_g[i], csh2, csc2)
            actc = _matmul(hc2, (w_gate, w_up), "swiglu", BF16, tn=256)
            cs = _matmul(actc, (w_down,), "resid", F32, extra=(cs, cg2), tm=512, tn=256)
    return xs[None]
```

```python
import functools
import math

import numpy as np
import jax
import jax.numpy as jnp
from jax import lax
from jax.experimental import pallas as pl
from jax.experimental.pallas import tpu as pltpu

HEAD_DIM = 128
GRID_W = 64
GQA_GROUP = 4
N_MOD = 6
N_MIXERS = 3
MIX_HYENA, MIX_GQA, MIX_DIFF = 0, 1, 2
NORM_EPS = 1e-6
ROPE_THETA = 10000.0
HY_ORDER = 2
HY_EMB_DIM = 33
HY_DECAY_TARGET = 1e-2
HY_FAST_DECAY_PCT = 0.3
HY_SLOW_DECAY_PCT = 1.5
ATTN_Q_SCALE = HEAD_DIM ** -0.5 * math.log2(math.e)
V7X_MXU_WIDTH = 256
ATTN_COL_TILE = V7X_MXU_WIDTH
ATTN_SOFTMAX_LAG = {"gqa": 3, "diff": 2}
ATTN_VALUE_LAG = {"gqa": 2, "diff": 1}
V7X_VMEM_BYTES = 64 * 1024 * 1024
VMEM_LIMIT_BYTES = V7X_VMEM_BYTES - 8 * 1024 * 1024
LANES = 128
SUBLANES = 8
BF16_ROWS = 16
DFT_ROW_LAG = 1
MM_ROW_CHUNK = 256

F32 = jnp.float32
BF16 = jnp.bfloat16
HIGHEST = lax.Precision.HIGHEST


def _pick(dim, pref, align):
    if dim <= pref:
        return dim
    best = None
    for d in range(align, pref + 1, align):
        if dim % d == 0:
            best = d
    assert best is not None, (dim, pref, align)
    return best


def _params(*sem):
    return pltpu.CompilerParams(dimension_semantics=sem, vmem_limit_bytes=VMEM_LIMIT_BYTES)


def _silu(x):
    return x * (1.0 / (1.0 + jnp.exp(-x)))


def _ada_kernel(cc_ref, a_ref, b_ref, bias_ref, o_ref):
    s = _silu(cc_ref[...])
    t = jnp.dot(s, a_ref[0], precision=HIGHEST, preferred_element_type=F32)
    o_ref[0] = jnp.dot(t, b_ref[0], precision=HIGHEST, preferred_element_type=F32) + bias_ref[0]


def _ada_mods(cc, ada_a, ada_b, ada_bias):
    depth, d, r = ada_a.shape
    n = ada_b.shape[2]
    tn = _pick(n, 2048, LANES)
    return pl.pallas_call(
        _ada_kernel,
        grid=(depth, n // tn),
        in_specs=[
            pl.BlockSpec((SUBLANES, d), lambda l, j: (0, 0)),
            pl.BlockSpec((1, d, r), lambda l, j: (l, 0, 0)),
            pl.BlockSpec((1, r, tn), lambda l, j: (l, 0, j)),
            pl.BlockSpec((1, 1, tn), lambda l, j: (l, 0, j)),
        ],
        out_specs=pl.BlockSpec((1, SUBLANES, tn), lambda l, j: (l, 0, j)),
        out_shape=jax.ShapeDtypeStruct((depth, SUBLANES, n), F32),
        compiler_params=_params("parallel", "parallel"),
        name="ada_mods",
    )(cc, ada_a, ada_b, ada_bias.reshape(depth, 1, n))


def _norm_mod_kernel(x_ref, g_ref, sh_ref, sc_ref, o_ref):
    x = x_ref[...]
    ms = jnp.mean(x * x, axis=-1, keepdims=True)
    y = x * lax.rsqrt(ms + NORM_EPS) * g_ref[...]
    o_ref[...] = (y * (1.0 + sc_ref[...]) + sh_ref[...]).astype(o_ref.dtype)


def _norm_mod(x, gain, shift, scale):
    m, d = x.shape
    tr = _pick(m, 256, SUBLANES)
    vec = pl.BlockSpec((1, d), lambda i: (0, 0))
    return pl.pallas_call(
        _norm_mod_kernel,
        grid=(m // tr,),
        in_specs=[pl.BlockSpec((tr, d), lambda i: (i, 0)), vec, vec, vec],
        out_specs=pl.BlockSpec((tr, d), lambda i: (i, 0)),
        out_shape=jax.ShapeDtypeStruct((m, d), BF16),
        compiler_params=_params("parallel"),
        name="norm_mod",
    )(x, gain.reshape(1, d), shift.reshape(1, d), scale.reshape(1, d))


def _head_norm_rope(y, gain, cos, sin, out_scale):
    outs = []
    if cos is not None:
        even = lax.broadcasted_iota(jnp.int32, cos.shape, 1) % 2 == 0
    for h in range(y.shape[1] // HEAD_DIM):
        t = y[:, h * HEAD_DIM:(h + 1) * HEAD_DIM]
        ms = jnp.mean(t * t, axis=-1, keepdims=True)
        t = t * lax.rsqrt(ms + NORM_EPS) * gain
        if cos is not None:
            partner = jnp.where(even, pltpu.roll(t, HEAD_DIM - 1, axis=1), pltpu.roll(t, 1, axis=1))
            t = t * cos + partner * sin
        if out_scale != 1.0:
            t = t * out_scale
        outs.append(t)
    return outs[0] if len(outs) == 1 else jnp.concatenate(outs, axis=1)


def _mm_kernel(*refs, mode, nk, out_scale):
    n_w = 2 if mode == "swiglu" else 1
    a_ref, w_refs = refs[0], refs[1:1 + n_w]
    n_extra = {"bias": 1, "plain": 0, "swiglu": 0, "resid": 2, "headnorm": 1, "headrope": 3}[mode]
    extra = refs[1 + n_w:1 + n_w + n_extra]
    o_ref = refs[1 + n_w + n_extra]
    accs = refs[2 + n_w + n_extra:]
    k = pl.program_id(2)

    def epilogue(ys, rows=slice(None)):
        y = ys[0]
        if mode == "bias":
            y = y + extra[0][...]
        elif mode == "swiglu":
            y = _silu(y) * ys[1]
        elif mode == "resid":
            y = extra[0][rows, :] + extra[1][...] * y
        elif mode == "headnorm":
            y = _head_norm_rope(y, extra[0][...], None, None, out_scale)
        elif mode == "headrope":
            y = _head_norm_rope(y, extra[0][...], extra[1][rows, :], extra[2][rows, :], out_scale)
        o_ref[rows, :] = y.astype(o_ref.dtype)

    if nk == 1:
        tm = a_ref.shape[0]
        rc = MM_ROW_CHUNK if tm % MM_ROW_CHUNK == 0 else tm
        pending = None
        for r in range(0, tm, rc):
            rows = slice(r, r + rc)
            a = a_ref[rows, :]
            parts = [jnp.dot(a, w[...], preferred_element_type=F32) for w in w_refs]
            if pending is not None:
                epilogue(*pending)
            pending = (parts, rows)
        epilogue(*pending)
        return

    a = a_ref[...]
    parts = [jnp.dot(a, w[...], preferred_element_type=F32) for w in w_refs]

    @pl.when(k == 0)
    def _():
        for acc, p in zip(accs, parts):
            acc[...] = p

    @pl.when(jnp.logical_and(k > 0, k < nk - 1))
    def _():
        for acc, p in zip(accs, parts):
            acc[...] += p

    @pl.when(k == nk - 1)
    def _():
        epilogue([acc[...] + p for acc, p in zip(accs, parts)])


def _matmul(a, ws, mode, out_dtype, *, extra=(), tm=1024, tn=512, tk=None, out_scale=1.0):
    m, kdim = a.shape
    n = ws[0].shape[1]
    tm = _pick(m, tm, BF16_ROWS)
    tn = _pick(n, tn, LANES)
    tk = kdim if tk is None else _pick(kdim, tk, LANES)
    nk = kdim // tk
    grid = (m // tm, n // tn, nk)
    in_specs = [pl.BlockSpec((tm, tk), lambda i, j, k: (i, k))]
    in_specs += [pl.BlockSpec((tk, tn), lambda i, j, k: (k, j)) for _ in ws]
    row = pl.BlockSpec((1, tn), lambda i, j, k: (0, j))
    head = pl.BlockSpec((1, HEAD_DIM), lambda i, j, k: (0, 0))
    ops = []
    if mode == "bias":
        in_specs += [row]
        ops = [extra[0].reshape(1, n)]
    elif mode == "resid":
        in_specs += [pl.BlockSpec((tm, tn), lambda i, j, k: (i, j)), row]
        ops = [extra[0], extra[1].reshape(1, n)]
    elif mode == "headnorm":
        in_specs += [head]
        ops = [extra[0].reshape(1, HEAD_DIM)]
    elif mode == "headrope":
        tab = pl.BlockSpec((tm, HEAD_DIM), lambda i, j, k: (i, 0))
        in_specs += [head, tab, tab]
        ops = [extra[0].reshape(1, HEAD_DIM), extra[1], extra[2]]
    n_acc = len(ws) if nk > 1 else 0
    return pl.pallas_call(
        functools.partial(_mm_kernel, mode=mode, nk=nk, out_scale=out_scale),
        grid=grid,
        in_specs=in_specs,
        out_specs=pl.BlockSpec((tm, tn), lambda i, j, k: (i, j)),
        out_shape=jax.ShapeDtypeStruct((m, n), out_dtype),
        scratch_shapes=[pltpu.VMEM((tm, tn), F32) for _ in range(n_acc)],
        compiler_params=_params("parallel", "parallel", "arbitrary"),
        name="mm_" + mode,
    )(a, *ws, *ops)


def _attn_kernel(*refs, kind, n_kv, lam_init):
    if kind == "gqa":
        q_ref, k_ref, vt_ref, o_ref, m_sc, acc_sc = refs
        n_q = GQA_GROUP
    else:
        q_ref, k_ref, vt_ref, lq1, lk1, lq2, lk2, sg_ref, o_ref, m_sc, acc_sc = refs
        n_q = 2
    kv = pl.program_id(2)
    dv = vt_ref.shape[0] - BF16_ROWS

    @pl.when(kv == 0)
    def _():
        m_sc[...] = jnp.full(m_sc.shape, -jnp.inf, F32)
        acc_sc[...] = jnp.zeros(acc_sc.shape, F32)

    vt = vt_ref[...]
    tq = q_ref.shape[0]
    tc = min(tq, ATTN_COL_TILE)
    units = [(h, c) for h in range(n_q) for c in range(tq // tc)]

    def scores_t(u):
        h, c = units[u]
        q = q_ref[c * tc:(c + 1) * tc, h * HEAD_DIM:(h + 1) * HEAD_DIM]
        kh = 0 if kind == "gqa" else h
        k = k_ref[:, kh * HEAD_DIM:(kh + 1) * HEAD_DIM]
        return lax.dot_general(k, q, (((1,), (1,)), ((), ())), preferred_element_type=F32)

    def softmax_t(u, st):
        h, c = units[u]
        m_prev = m_sc[h, :, c * tc:(c + 1) * tc]
        m_new = jnp.maximum(m_prev, jnp.max(st, axis=0, keepdims=True))
        m_sc[h, :, c * tc:(c + 1) * tc] = m_new
        return jnp.exp2(st - m_new).astype(vt.dtype), jnp.exp2(m_prev - m_new)

    def accumulate(u, p, alpha):
        h, c = units[u]
        acc = acc_sc[h, :, c * tc:(c + 1) * tc]
        acc_sc[h, :, c * tc:(c + 1) * tc] = alpha * acc + jnp.dot(vt, p, preferred_element_type=F32)

    st, pa = {}, {}
    lag_b, lag_c = ATTN_SOFTMAX_LAG[kind], ATTN_SOFTMAX_LAG[kind] + ATTN_VALUE_LAG[kind]
    for i in range(len(units) + lag_c):
        if i < len(units):
            st[i] = scores_t(i)
        if 0 <= i - lag_b < len(units):
            pa[i - lag_b] = softmax_t(i - lag_b, st.pop(i - lag_b))
        if 0 <= i - lag_c < len(units):
            accumulate(i - lag_c, *pa.pop(i - lag_c))

    @pl.when(kv == n_kv - 1)
    def _():
        def normalised(h):
            acc = acc_sc[h]
            return acc[:dv] / acc[dv:dv + 1]

        if kind == "gqa":
            for h in range(n_q):
                o_ref[:, h * HEAD_DIM:(h + 1) * HEAD_DIM] = normalised(h).T.astype(o_ref.dtype)
        else:
            lam = (jnp.exp(jnp.sum(lq1[...] * lk1[...], axis=-1, keepdims=True))
                   - jnp.exp(jnp.sum(lq2[...] * lk2[...], axis=-1, keepdims=True)) + lam_init)
            o = (normalised(0) - lam * normalised(1)).T
            ms = jnp.mean(o * o, axis=-1, keepdims=True)
            o = o * lax.rsqrt(ms + NORM_EPS) * sg_ref[...] * (1.0 - lam_init)
            o_ref[...] = o.astype(o_ref.dtype)


def _attention(q, k, v, kind, *, lam_params=None, sub_gain=None, lam_init=0.0):
    lq, d = q.shape
    t = k.shape[0]
    if kind == "gqa":
        n_heads = d // (GQA_GROUP * HEAD_DIM)
        qw, kw, vw, n_q = GQA_GROUP * HEAD_DIM, HEAD_DIM, HEAD_DIM, GQA_GROUP
    else:
        n_heads = d // (2 * HEAD_DIM)
        qw, kw, vw, n_q = 2 * HEAD_DIM, 2 * HEAD_DIM, 2 * HEAD_DIM, 2
    tq = _pick(lq, 1024, ATTN_COL_TILE)
    tk = _pick(t, 1280, LANES)
    n_kv = t // tk
    vt = v.T.reshape(v.shape[1] // vw, vw, t)
    ones = jnp.zeros((vt.shape[0], BF16_ROWS, t), v.dtype).at[:, 0, :].set(1.0)
    vt = jnp.concatenate([vt, ones], axis=1)
    vr = vw + BF16_ROWS
    in_specs = [
        pl.BlockSpec((tq, qw), lambda h, i, j: (i, h)),
        pl.BlockSpec((tk, kw), lambda h, i, j: (j, h)),
        pl.BlockSpec((None, vr, tk), lambda h, i, j: (h, 0, j)),
    ]
    ops = [q, k, vt]
    if kind == "diff":
        vec = pl.BlockSpec((1, HEAD_DIM), lambda h, i, j: (0, 0))
        in_specs += [vec, vec, vec, vec, pl.BlockSpec((1, 2 * HEAD_DIM), lambda h, i, j: (0, 0))]
        ops += [p.reshape(1, HEAD_DIM) for p in lam_params] + [sub_gain.reshape(1, 2 * HEAD_DIM)]
    return pl.pallas_call(
        functools.partial(_attn_kernel, kind=kind, n_kv=n_kv, lam_init=lam_init),
        grid=(n_heads, lq // tq, n_kv),
        in_specs=in_specs,
        out_specs=pl.BlockSpec((tq, qw), lambda h, i, j: (i, h)),
        out_shape=jax.ShapeDtypeStruct((lq, d), BF16),
        scratch_shapes=[
            pltpu.VMEM((n_q, 1, tq), F32),
            pltpu.VMEM((n_q, vr, tq), F32),
        ],
        compiler_params=_params("parallel", "parallel", "arbitrary"),
        name="attn_" + kind,
    )(*ops)


def _sconv_kernel(z_ref, zp_ref, zn_ref, w_ref, b_ref, o_ref, *, n_row_blocks):
    i = pl.program_id(1)
    z = z_ref[...]
    tr = z.shape[0]
    prev_row = jnp.where(i == 0, 0.0, zp_ref[SUBLANES - 1:SUBLANES, :])
    next_row = jnp.where(i == n_row_blocks - 1, 0.0, zn_ref[0:1, :])
    row = lax.broadcasted_iota(jnp.int32, z.shape, 0)
    z_before = jnp.where(row == 0, prev_row, pltpu.roll(z, 1, axis=0))
    z_after = jnp.where(row == tr - 1, next_row, pltpu.roll(z, tr - 1, axis=0))
    w = w_ref[...]
    o_ref[0] = b_ref[...] + z_before * w[0:1] + z * w[1:2] + z_after * w[2:3]


def _short_conv(z, conv_w, conv_b, n_split):
    l, c = z.shape
    d = c // n_split
    tr = _pick(l, 512, SUBLANES)
    tc = _pick(d, 1024, LANES)
    ncb = d // tc
    nrb = l // tr
    r8 = tr // SUBLANES
    last8 = l // SUBLANES - 1
    return pl.pallas_call(
        functools.partial(_sconv_kernel, n_row_blocks=nrb),
        grid=(n_split, nrb, ncb),
        in_specs=[
            pl.BlockSpec((tr, tc), lambda s, i, j: (i, s * ncb + j)),
            pl.BlockSpec((SUBLANES, tc), lambda s, i, j: (jnp.maximum(i * r8 - 1, 0), s * ncb + j)),
            pl.BlockSpec((SUBLANES, tc), lambda s, i, j: (jnp.minimum((i + 1) * r8, last8), s * ncb + j)),
            pl.BlockSpec((conv_w.shape[0], tc), lambda s, i, j: (0, s * ncb + j)),
            pl.BlockSpec((1, tc), lambda s, i, j: (0, s * ncb + j)),
        ],
        out_specs=pl.BlockSpec((1, tr, tc), lambda s, i, j: (s, i, j)),
        out_shape=jax.ShapeDtypeStruct((n_split, l, d), F32),
        compiler_params=_params("parallel", "parallel", "parallel"),
        name="short_conv",
    )(z, z, z, conv_w, conv_b.reshape(1, c))


def _filter_mlp_kernel(f_ref, w1_ref, b1_ref, w2_ref, b2_ref, fr_ref, o_ref):
    g = jnp.dot(f_ref[...], w1_ref[...], precision=HIGHEST, preferred_element_type=F32)
    g = jnp.sin(fr_ref[0:1] * (g + b1_ref[...]))
    g = jnp.dot(g, w2_ref[...], precision=HIGHEST, preferred_element_type=F32)
    o_ref[...] = jnp.sin(fr_ref[1:2] * (g + b2_ref[...]))


def _filter_taps_fft1_kernel(g_ref, w3f_ref, w3b_ref, dl_ref, f_ref, tc_ref, ts_ref,
                             are_ref, aim_ref, asum_ref, *, n_tok, n2_size, k1p):
    n2 = pl.program_id(1)
    n1_size = g_ref.shape[0]
    half = n1_size // 2
    g = g_ref[...]
    taps = jnp.concatenate([
        jnp.dot(g[:half], w3f_ref[...], precision=HIGHEST, preferred_element_type=F32),
        jnp.dot(g[half:], w3b_ref[...], precision=HIGHEST, preferred_element_type=F32)], axis=0)
    n = lax.broadcasted_iota(jnp.int32, (n1_size, 1), 0) * n2_size + n2
    tau = jnp.where(n < n_tok, n, 2 * n_tok - n)
    t = tau.astype(F32) * (1.0 / (n_tok - 1))
    taps = taps * jnp.exp(-t * dl_ref[...])
    taps = jnp.where(n == n_tok, 0.0, taps)

    @pl.when(n2 == 0)
    def _():
        asum_ref[...] = jnp.zeros(asum_ref.shape, F32)

    asum_ref[...] += jnp.sum(jnp.abs(taps), axis=0, keepdims=True)

    a = jnp.dot(f_ref[...], _split3(taps), preferred_element_type=F32)
    re, im = a[:k1p], a[k1p:]
    c = _table_column(tc_ref, n2)
    s = _table_column(ts_ref, n2)
    are_ref[...] = re * c + im * s
    aim_ref[...] = im * c - re * s


def _hyena_filter_stage1(n_tok, d, consts, f_w1, f_b1, f_w2, f_b2, f_w3, f_freq):
    width = f_w1.shape[1]
    t = jnp.linspace(0.0, 1.0, n_tok, dtype=F32)[:, None]
    bands = (HY_EMB_DIM - 1) // 2
    w = (2.0 * math.pi / n_tok) * jnp.arange(n_tok, dtype=F32)[:, None]
    f = jnp.linspace(1e-4, bands - 1, bands, dtype=F32)[None, :]
    feats = jnp.concatenate([t, jnp.cos(f * w), -jnp.sin(f * w)], axis=-1)
    feats2 = jnp.concatenate([feats, feats[:1], feats[:0:-1]], axis=0)
    feats2 = jnp.pad(feats2, ((0, 0), (0, LANES - HY_EMB_DIM)))
    pw = LANES - width
    w1 = jnp.pad(f_w1, ((0, LANES - HY_EMB_DIM), (0, pw)))
    w2 = jnp.pad(f_w2, ((0, pw), (0, pw)))
    b1 = jnp.pad(f_b1, (0, pw)).reshape(1, LANES)
    b2 = jnp.pad(f_b2, (0, pw)).reshape(1, LANES)
    freq = jnp.pad(f_freq, ((0, 0), (0, pw)))
    n2l = 2 * n_tok
    tr = _pick(n_tok, 512, SUBLANES)
    full = lambda shape: pl.BlockSpec(shape, lambda i: (0,) * len(shape))
    g = pl.pallas_call(
        _filter_mlp_kernel,
        grid=(n2l // tr,),
        in_specs=[pl.BlockSpec((tr, LANES), lambda i: (i, 0)), full((LANES, LANES)), full((1, LANES)),
                  full((LANES, LANES)), full((1, LANES)), full((2, LANES))],
        out_specs=pl.BlockSpec((tr, LANES), lambda i: (i, 0)),
        out_shape=jax.ShapeDtypeStruct((n2l, LANES), F32),
        compiler_params=_params("parallel"),
        name="hyena_filter_mlp",
    )(feats2, w1, b1, w2, b2, freq)

    c = HY_ORDER * d
    w3 = jnp.pad(f_w3, ((0, pw), (0, 0)))
    max_decay = math.log(HY_DECAY_TARGET) / HY_FAST_DECAY_PCT
    min_decay = math.log(HY_DECAY_TARGET) / HY_SLOW_DECAY_PCT
    deltas = jnp.abs(jnp.linspace(min_decay, max_decay, d, dtype=F32))
    deltas = jnp.tile(deltas, HY_ORDER).reshape(1, c)
    _, n1, n2, _, k1p = _dft_plan(n_tok)
    g_by_n2 = g.reshape(n1, n2, LANES).transpose(1, 0, 2)
    tcol = _pick(c, 2048, LANES)
    ncb = c // tcol
    tab = pl.BlockSpec((k1p, n2), lambda j, i: (0, 0))
    out = pl.BlockSpec((k1p, tcol), lambda j, i: (0, i * ncb + j))
    return pl.pallas_call(
        functools.partial(_filter_taps_fft1_kernel, n_tok=n_tok, n2_size=n2, k1p=k1p),
        grid=(ncb, n2),
        in_specs=[
            pl.BlockSpec((None, n1, LANES), lambda j, i: (i, 0, 0)),
            pl.BlockSpec((LANES, tcol), lambda j, i: (0, j)),
            pl.BlockSpec((LANES, tcol), lambda j, i: (0, ncb + j)),
            pl.BlockSpec((1, tcol), lambda j, i: (0, j)),
            pl.BlockSpec((2 * k1p, 3 * n1), lambda j, i: (0, 0)), tab, tab,
        ],
        out_specs=[out, out, pl.BlockSpec((1, tcol), lambda j, i: (0, j))],
        out_shape=[jax.ShapeDtypeStruct((k1p, n2 * c), F32)] * 2 + [jax.ShapeDtypeStruct((1, c), F32)],
        compiler_params=_params("parallel", "arbitrary"),
        name="hyena_filter_fft1",
    )(g_by_n2, w3, w3, deltas, consts["f1_full"], consts["tw_cos"], consts["tw_sin"])


def _split3(x):
    hi = x.astype(BF16)
    lo = (x - hi.astype(F32)).astype(BF16)
    return jnp.concatenate([hi, lo, hi], axis=0)


def _mat3(m64):
    m = jnp.asarray(m64, F32)
    hi = m.astype(BF16)
    lo = (m - hi.astype(F32)).astype(BF16)
    return jnp.concatenate([hi, hi, lo], axis=1)


def _dft_plan(n_tok):
    n = 2 * n_tok
    n1 = 1 << ((n.bit_length() - 1 + 1) // 2)
    n2 = n // n1
    assert n1 * n2 == n and n2 >= SUBLANES, (n_tok, n1, n2)
    k1 = n1 // 2 + 1
    k1p = -(-k1 // BF16_ROWS) * BF16_ROWS
    return n, n1, n2, k1, k1p


def _dft_constants(n_tok):
    n, n1, n2, k1, k1p = _dft_plan(n_tok)
    two_pi = 2.0 * np.pi
    kk = np.arange(k1p)[:, None]
    valid = (kk < k1).astype(np.float64)
    ang = two_pi * ((kk * np.arange(n1)[None, :]) % n1) / n1
    f1 = np.concatenate([np.cos(ang) * valid, -np.sin(ang) * valid], axis=0)
    a2 = two_pi * ((np.arange(n2)[:, None] * np.arange(n2)[None, :]) % n2) / n2
    c2, s2 = np.cos(a2), np.sin(a2)
    f2 = np.block([[c2, s2], [-s2, c2]])
    g2 = np.block([[c2, -s2], [s2, c2]])
    wk = np.where((np.arange(k1p) == 0) | (np.arange(k1p) == n1 // 2), 1.0, 2.0) * (np.arange(k1p) < k1)
    a3 = two_pi * ((np.arange(n1 // 2)[:, None] * np.arange(k1p)[None, :]) % n1) / n1
    g3 = np.concatenate([np.cos(a3) * wk[None, :], -np.sin(a3) * wk[None, :]], axis=1) / n
    at = two_pi * (kk * np.arange(n2)[None, :]) / n
    return dict(
        f1_full=_mat3(f1), f1_half=_mat3(f1[:, :n1 // 2]), f2=_mat3(f2), g2=_mat3(g2), g3=_mat3(g3),
        tw_cos=jnp.asarray(np.cos(at), F32), tw_sin=jnp.asarray(np.sin(at), F32),
        tw_cos_t=jnp.asarray(np.cos(at).T, F32), tw_sin_t=jnp.asarray(np.sin(at).T, F32),
    )


def _table_column(tab_ref, idx):
    lane = lax.broadcasted_iota(jnp.int32, tab_ref.shape, 1)
    return jnp.sum(jnp.where(lane == idx, tab_ref[...], 0.0), axis=1, keepdims=True)


def _fft1_kernel(x_ref, f_ref, tc_ref, ts_ref, are_ref, aim_ref, *, k1p, blocks_per_n2):
    n2 = pl.program_id(0) // blocks_per_n2
    a = jnp.dot(f_ref[...], _split3(x_ref[...]), preferred_element_type=F32)
    re, im = a[:k1p], a[k1p:]
    c = _table_column(tc_ref, n2)
    s = _table_column(ts_ref, n2)
    are_ref[...] = re * c + im * s
    aim_ref[...] = im * c - re * s


def _fft_stage1(x2d, f1, consts, n_tok, n_ch):
    _, n1, n2, _, k1p = _dft_plan(n_tok)
    n1c, cols = x2d.shape
    tcol = _pick(n_ch, 2048, LANES)
    tab = pl.BlockSpec((k1p, n2), lambda j: (0, 0))
    out = pl.BlockSpec((k1p, tcol), lambda j: (0, j))
    return pl.pallas_call(
        functools.partial(_fft1_kernel, k1p=k1p, blocks_per_n2=n_ch // tcol),
        grid=(cols // tcol,),
        in_specs=[pl.BlockSpec((n1c, tcol), lambda j: (0, j)),
                  pl.BlockSpec((2 * k1p, 3 * n1c), lambda j: (0, 0)), tab, tab],
        out_specs=[out, out],
        out_shape=[jax.ShapeDtypeStruct((k1p, cols), F32)] * 2,
        compiler_params=_params("parallel"),
        name="hyena_fft1",
    )(x2d, f1, consts["tw_cos"], consts["tw_sin"])


def _fft2_spec_kernel(are_ref, aim_ref, f_ref, asum_ref, sre_ref, sim_ref, *, tk1, n2):
    inv = 1.0 / (asum_ref[...] + NORM_EPS)

    def forward(r):
        a = jnp.concatenate([are_ref[r], aim_ref[r]], axis=0)
        return jnp.dot(f_ref[...], _split3(a), preferred_element_type=F32)

    def store(r, x):
        sre_ref[r] = x[:n2] * inv
        sim_ref[r] = x[n2:] * inv

    xs = {}
    for r in range(tk1 + DFT_ROW_LAG):
        if r < tk1:
            xs[r] = forward(r)
        if r >= DFT_ROW_LAG:
            store(r - DFT_ROW_LAG, xs.pop(r - DFT_ROW_LAG))


def _fft2_conv_kernel(are_ref, aim_ref, sre_ref, sim_ref, f_ref, g_ref, tc_ref, ts_ref,
                      bre_ref, bim_ref, *, tk1, n2):
    i = pl.program_id(0)

    def forward(r):
        a = jnp.concatenate([are_ref[r], aim_ref[r]], axis=0)
        return jnp.dot(f_ref[...], _split3(a), preferred_element_type=F32)

    def inverse(r, x):
        xr, xi = x[:n2], x[n2:]
        sr, si = sre_ref[r], sim_ref[r]
        y = jnp.concatenate([xr * sr - xi * si, xr * si + xi * sr], axis=0)
        return jnp.dot(g_ref[...], _split3(y), preferred_element_type=F32)

    def store(r, b):
        br, bi = b[:n2], b[n2:]
        c = _table_column(tc_ref, i * tk1 + r)
        s = _table_column(ts_ref, i * tk1 + r)
        bre_ref[r] = br * c - bi * s
        bim_ref[r] = bi * c + br * s

    xs, bs = {}, {}
    for r in range(tk1 + 2 * DFT_ROW_LAG):
        if r < tk1:
            xs[r] = forward(r)
        if 0 <= r - DFT_ROW_LAG < tk1:
            bs[r - DFT_ROW_LAG] = inverse(r - DFT_ROW_LAG, xs.pop(r - DFT_ROW_LAG))
        if 0 <= r - 2 * DFT_ROW_LAG < tk1:
            store(r - 2 * DFT_ROW_LAG, bs.pop(r - 2 * DFT_ROW_LAG))


def _fft_stage2(are, aim, consts, n_tok, n_ch, *, spec=None, asum=None, spec_offset=0):
    _, n1, n2, _, k1p = _dft_plan(n_tok)
    tk1 = SUBLANES
    td = _pick(n_ch, 512, LANES)
    blk = pl.BlockSpec((tk1, n2, td), lambda i, j: (i, 0, j))
    mat = pl.BlockSpec((2 * n2, 6 * n2), lambda i, j: (0, 0))
    grid = (k1p // tk1, n_ch // td)
    out_shape = [jax.ShapeDtypeStruct((k1p, n2, n_ch), F32)] * 2
    if spec is None:
        return pl.pallas_call(
            functools.partial(_fft2_spec_kernel, tk1=tk1, n2=n2),
            grid=grid,
            in_specs=[blk, blk, mat, pl.BlockSpec((1, td), lambda i, j: (0, j))],
            out_specs=[blk, blk],
            out_shape=out_shape,
            compiler_params=_params("parallel", "parallel"),
            name="hyena_fft2_spec",
        )(are, aim, consts["f2"], asum)
    off = spec_offset // td
    sblk = pl.BlockSpec((tk1, n2, td), lambda i, j: (i, 0, j + off))
    tab = pl.BlockSpec((n2, k1p), lambda i, j: (0, 0))
    return pl.pallas_call(
        functools.partial(_fft2_conv_kernel, tk1=tk1, n2=n2),
        grid=grid,
        in_specs=[blk, blk, sblk, sblk, mat, mat, tab, tab],
        out_specs=[blk, blk],
        out_shape=out_shape,
        compiler_params=_params("parallel", "parallel"),
        name="hyena_fft2_conv",
    )(are, aim, spec[0], spec[1], consts["f2"], consts["g2"], consts["tw_cos_t"], consts["tw_sin_t"])


def _fft3_kernel(bre_ref, bim_ref, g_ref, y_ref, gate_ref, bias_ref, o_ref):
    b = jnp.concatenate([bre_ref[...], bim_ref[...]], axis=0)
    conv = jnp.dot(g_ref[...], _split3(b), preferred_element_type=F32)
    o_ref[...] = (gate_ref[...] * (conv + y_ref[...] * bias_ref[...])).astype(o_ref.dtype)


def _fft_stage3(bre, bim, consts, n_tok, n_ch, y2d, gate2d, bias, out_dtype):
    _, n1, n2, _, k1p = _dft_plan(n_tok)
    cols = n2 * n_ch
    tcol = _pick(n_ch, 2048, LANES)
    per = n_ch // tcol
    inb = pl.BlockSpec((k1p, tcol), lambda j: (0, j))
    sig = pl.BlockSpec((n1 // 2, tcol), lambda j: (0, j))
    return pl.pallas_call(
        _fft3_kernel,
        grid=(cols // tcol,),
        in_specs=[inb, inb, pl.BlockSpec((n1 // 2, 6 * k1p), lambda j: (0, 0)), sig, sig,
                  pl.BlockSpec((1, tcol), lambda j: (0, j % per))],
        out_specs=sig,
        out_shape=jax.ShapeDtypeStruct((n1 // 2, cols), out_dtype),
        compiler_params=_params("parallel"),
        name="hyena_fft3",
    )(bre, bim, consts["g3"], y2d, gate2d, bias.reshape(1, n_ch))


def _hyena_mixer_core(h, w_in, b_in, conv_w, conv_b, f_w1, f_b1, f_w2, f_b2, f_w3, f_freq, f_bias):
    n_tok, d = h.shape
    _, n1, n2, _, k1p = _dft_plan(n_tok)
    consts = _dft_constants(n_tok)
    z = _matmul(h, (w_in,), "bias", F32, extra=(b_in,))
    zs = _short_conv(z, conv_w, conv_b, HY_ORDER + 1)

    c = HY_ORDER * d
    tre, tim, asum = _hyena_filter_stage1(n_tok, d, consts, f_w1, f_b1, f_w2, f_b2, f_w3, f_freq)
    spec = _fft_stage2(tre.reshape(k1p, n2, c), tim.reshape(k1p, n2, c), consts, n_tok, c, asum=asum)

    y = zs[0].reshape(n1 // 2, n2 * d)
    for n in range(HY_ORDER):
        are, aim = _fft_stage1(y, consts["f1_half"], consts, n_tok, d)
        bre, bim = _fft_stage2(are.reshape(k1p, n2, d), aim.reshape(k1p, n2, d), consts, n_tok, d,
                               spec=spec, spec_offset=n * d)
        last = n == HY_ORDER - 1
        y = _fft_stage3(bre.reshape(k1p, n2 * d), bim.reshape(k1p, n2 * d), consts, n_tok, d, y,
                        zs[n + 1].reshape(n1 // 2, n2 * d), f_bias[n], BF16 if last else F32)
    return y.reshape(n_tok, d)


def _axial_rope_tables(n_tok):
    rows = n_tok // GRID_W
    row = jnp.repeat(jnp.arange(rows, dtype=F32), GRID_W)
    col = (jnp.arange(rows * GRID_W) % GRID_W).astype(F32)
    n_pair = HEAD_DIM // 4
    inv_freq = ROPE_THETA ** (-jnp.arange(n_pair, dtype=F32) / n_pair)
    ang = jnp.concatenate([row[:, None] * inv_freq, col[:, None] * inv_freq], axis=-1)
    cos, sin = jnp.cos(ang), jnp.sin(ang)
    signed_sin = jnp.stack([-sin, sin], axis=-1).reshape(sin.shape[0], HEAD_DIM)
    return jnp.repeat(cos, 2, axis=-1), signed_sin


def kernel(x, c, ctx, c_ctx, ada_a, ada_b, ada_bias, norm1_g, norm2_g, ffn_w_gate, ffn_w_up, ffn_w_down,
           hy_w_in, hy_b_in, hy_conv_w, hy_conv_b, hy_f_w1, hy_f_b1, hy_f_w2, hy_f_b2, hy_f_w3, hy_f_freq,
           hy_f_bias, hy_w_out, gqa_w_q, gqa_w_k, gqa_w_v, gqa_q_norm, gqa_k_norm, gqa_w_o,
           diff_w_q, diff_w_k, diff_w_v, diff_q_norm, diff_k_norm, diff_lambda_q1, diff_lambda_k1,
           diff_lambda_q2, diff_lambda_k2, diff_sub_norm, diff_w_o):
    assert x.shape[0] == 1 and ctx.shape[0] == 1
    depth = ada_a.shape[0]
    n_tok, d = x.shape[1], x.shape[2]
    xs = x[0]
    cs = ctx[0]
    cos, sin = _axial_rope_tables(n_tok)

    cc = jnp.concatenate([c.reshape(1, d), c_ctx.reshape(1, d), jnp.zeros((SUBLANES - 2, d), F32)], axis=0)
    mods = _ada_mods(cc, ada_a, ada_b, ada_bias)

    kinds = [i % N_MIXERS for i in range(depth)]
    for i in range(depth):
        kind = kinds[i]
        j = i // N_MIXERS
        reads_ctx = kind != MIX_HYENA
        update_ctx = any(kk != MIX_HYENA for kk in kinds[i + 1:])
        sh1, sc1, g1, sh2, sc2, g2 = [mods[i, 0, m * d:(m + 1) * d] for m in range(N_MOD)]
        csh1, csc1, cg1, csh2, csc2, cg2 = [mods[i, 1, m * d:(m + 1) * d] for m in range(N_MOD)]
        h = _norm_mod(xs, norm1_g[i], sh1, sc1)
        hc = _norm_mod(cs, norm1_g[i], csh1, csc1) if (reads_ctx or update_ctx) else None
        yc = None
        if kind == MIX_HYENA:
            w_in = hy_w_in[j].astype(BF16)
            w_out = hy_w_out[j].astype(BF16)
            args = (w_in, hy_b_in[j], hy_conv_w[j], hy_conv_b[j], hy_f_w1[j], hy_f_b1[j], hy_f_w2[j],
                    hy_f_b2[j], hy_f_w3[j], hy_f_freq[j], hy_f_bias[j])
            y = _hyena_mixer_core(h, *args)
            xs = _matmul(y, (w_out,), "resid", F32, extra=(xs, g1))
            if update_ctx:
                yc = _hyena_mixer_core(hc, *args)
                cs = _matmul(yc, (w_out,), "resid", F32, extra=(cs, cg1))
        elif kind == MIX_GQA:
            w_q = gqa_w_q[j].astype(BF16)
            w_k = gqa_w_k[j].astype(BF16)
            w_v = gqa_w_v[j].astype(BF16)
            w_o = gqa_w_o[j].astype(BF16)
            qg, kg = gqa_q_norm[j], gqa_k_norm[j]
            q = _matmul(h, (w_q,), "headrope", BF16, extra=(qg, cos, sin), out_scale=ATTN_Q_SCALE)
            k = _matmul(h, (w_k,), "headrope", BF16, extra=(kg, cos, sin))
            v = _matmul(h, (w_v,), "plain", BF16)
            kc = _matmul(hc, (w_k,), "headnorm", BF16, extra=(kg,))
            vc = _matmul(hc, (w_v,), "plain", BF16)
            o = _attention(q, jnp.concatenate([k, kc], axis=0), jnp.concatenate([v, vc], axis=0), "gqa")
            xs = _matmul(o, (w_o,), "resid", F32, extra=(xs, g1))
            if update_ctx:
                qc = _matmul(hc, (w_q,), "headnorm", BF16, extra=(qg,), out_scale=ATTN_Q_SCALE)
                oc = _attention(qc, kc, vc, "gqa")
                cs = _matmul(oc, (w_o,), "resid", F32, extra=(cs, cg1))
        else:
            w_q = diff_w_q[j].astype(BF16)
            w_k = diff_w_k[j].astype(BF16)
            w_v = diff_w_v[j].astype(BF16)
            w_o = diff_w_o[j].astype(BF16)
            qg, kg = diff_q_norm[j], diff_k_norm[j]
            lam_init = 0.8 - 0.6 * math.exp(-0.3 * i)
            lam_params = (diff_lambda_q1[j], diff_lambda_k1[j], diff_lambda_q2[j], diff_lambda_k2[j])
            attend = functools.partial(_attention, kind="diff", lam_params=lam_params,
                                       sub_gain=diff_sub_norm[j], lam_init=lam_init)
            q = _matmul(h, (w_q,), "headrope", BF16, extra=(qg, cos, sin), out_scale=ATTN_Q_SCALE)
            k = _matmul(h, (w_k,), "headrope", BF16, extra=(kg, cos, sin))
            v = _matmul(h, (w_v,), "plain", BF16)
            kc = _matmul(hc, (w_k,), "headnorm", BF16, extra=(kg,))
            vc = _matmul(hc, (w_v,), "plain", BF16)
            o = attend(q, jnp.concatenate([k, kc], axis=0), jnp.concatenate([v, vc], axis=0))
            xs = _matmul(o, (w_o,), "resid", F32, extra=(xs, g1))
            if update_ctx:
                qc = _matmul(hc, (w_q,), "headnorm", BF16, extra=(qg,), out_scale=ATTN_Q_SCALE)
                oc = attend(qc, kc, vc)
                cs = _matmul(oc, (w_o,), "resid", F32, extra=(cs, cg1))

        w_gate = ffn_w_gate[i].astype(BF16)
        w_up = ffn_w_up[i].astype(BF16)
        w_down = ffn_w_down[i].astype(BF16)
        h2 = _norm_mod(xs, norm2_g[i], sh2, sc2)
        act = _matmul(h2, (w_gate, w_up), "swiglu", BF16, tn=256)
        xs = _matmul(act, (w_down,), "resid", F32, extra=(xs, g2), tm=512, tn=256)
        if update_ctx:
            hc2 = _norm_mod(cs, norm2_g[i], csh2, csc2)
            actc = _matmul(hc2, (w_gate, w_up), "swiglu", BF16, tn=256)
            cs = _matmul(actc, (w_down,), "resid", F32, extra=(cs, cg2), tm=512, tn=256)
    return xs[None]
```

```python
import functools
import math

import numpy as np
import jax
import jax.numpy as jnp
from jax import lax
from jax.experimental import pallas as pl
from jax.experimental.pallas import tpu as pltpu

HEAD_DIM = 128
GRID_W = 64
GQA_GROUP = 4
N_MOD = 6
N_MIXERS = 3
MIX_HYENA, MIX_GQA, MIX_DIFF = 0, 1, 2
NORM_EPS = 1e-6
ROPE_THETA = 10000.0
HY_ORDER = 2
HY_EMB_DIM = 33
HY_DECAY_TARGET = 1e-2
HY_FAST_DECAY_PCT = 0.3
HY_SLOW_DECAY_PCT = 1.5
ATTN_Q_SCALE = HEAD_DIM ** -0.5 * math.log2(math.e)
V7X_MXU_WIDTH = 256
ATTN_COL_TILE = V7X_MXU_WIDTH
ATTN_KEY_TILE = {"gqa": 1280, "diff": 3328}
ATTN_SOFTMAX_LAG = {"gqa": 3, "diff": 2}
ATTN_VALUE_LAG = {"gqa": 2, "diff": 1}
V7X_VMEM_BYTES = 64 * 1024 * 1024
VMEM_LIMIT_BYTES = V7X_VMEM_BYTES - 8 * 1024 * 1024
LANES = 128
SUBLANES = 8
BF16_ROWS = 16
DFT_ROW_LAG = 1
MM_ROW_CHUNK = 256

F32 = jnp.float32
BF16 = jnp.bfloat16
HIGHEST = lax.Precision.HIGHEST


def _pick(dim, pref, align):
    if dim <= pref:
        return dim
    best = None
    for d in range(align, pref + 1, align):
        if dim % d == 0:
            best = d
    assert best is not None, (dim, pref, align)
    return best


def _params(*sem):
    return pltpu.CompilerParams(dimension_semantics=sem, vmem_limit_bytes=VMEM_LIMIT_BYTES)


def _silu(x):
    return x * (1.0 / (1.0 + jnp.exp(-x)))


def _ada_kernel(cc_ref, a_ref, b_ref, bias_ref, o_ref):
    s = _silu(cc_ref[...])
    t = jnp.dot(s, a_ref[0], precision=HIGHEST, preferred_element_type=F32)
    o_ref[0] = jnp.dot(t, b_ref[0], precision=HIGHEST, preferred_element_type=F32) + bias_ref[0]


def _ada_mods(cc, ada_a, ada_b, ada_bias):
    depth, d, r = ada_a.shape
    n = ada_b.shape[2]
    tn = _pick(n, 2048, LANES)
    return pl.pallas_call(
        _ada_kernel,
        grid=(depth, n // tn),
        in_specs=[
            pl.BlockSpec((SUBLANES, d), lambda l, j: (0, 0)),
            pl.BlockSpec((1, d, r), lambda l, j: (l, 0, 0)),
            pl.BlockSpec((1, r, tn), lambda l, j: (l, 0, j)),
            pl.BlockSpec((1, 1, tn), lambda l, j: (l, 0, j)),
        ],
        out_specs=pl.BlockSpec((1, SUBLANES, tn), lambda l, j: (l, 0, j)),
        out_shape=jax.ShapeDtypeStruct((depth, SUBLANES, n), F32),
        compiler_params=_params("parallel", "parallel"),
        name="ada_mods",
    )(cc, ada_a, ada_b, ada_bias.reshape(depth, 1, n))


def _norm_mod_kernel(x_ref, g_ref, sh_ref, sc_ref, o_ref):
    x = x_ref[...]
    ms = jnp.mean(x * x, axis=-1, keepdims=True)
    y = x * lax.rsqrt(ms + NORM_EPS) * g_ref[...]
    o_ref[...] = (y * (1.0 + sc_ref[...]) + sh_ref[...]).astype(o_ref.dtype)


def _norm_mod(x, gain, shift, scale):
    m, d = x.shape
    tr = _pick(m, 256, SUBLANES)
    vec = pl.BlockSpec((1, d), lambda i: (0, 0))
    return pl.pallas_call(
        _norm_mod_kernel,
        grid=(m // tr,),
        in_specs=[pl.BlockSpec((tr, d), lambda i: (i, 0)), vec, vec, vec],
        out_specs=pl.BlockSpec((tr, d), lambda i: (i, 0)),
        out_shape=jax.ShapeDtypeStruct((m, d), BF16),
        compiler_params=_params("parallel"),
        name="norm_mod",
    )(x, gain.reshape(1, d), shift.reshape(1, d), scale.reshape(1, d))


def _head_norm_rope(y, gain, cos, sin, out_scale):
    outs = []
    if cos is not None:
        even = lax.broadcasted_iota(jnp.int32, cos.shape, 1) % 2 == 0
    for h in range(y.shape[1] // HEAD_DIM):
        t = y[:, h * HEAD_DIM:(h + 1) * HEAD_DIM]
        ms = jnp.mean(t * t, axis=-1, keepdims=True)
        t = t * lax.rsqrt(ms + NORM_EPS) * gain
        if cos is not None:
            partner = jnp.where(even, pltpu.roll(t, HEAD_DIM - 1, axis=1), pltpu.roll(t, 1, axis=1))
            t = t * cos + partner * sin
        if out_scale != 1.0:
            t = t * out_scale
        outs.append(t)
    return outs[0] if len(outs) == 1 else jnp.concatenate(outs, axis=1)


def _mm_kernel(*refs, mode, nk, out_scale, head_width):
    n_w = 2 if mode == "swiglu" else 1
    a_ref, w_refs = refs[0], refs[1:1 + n_w]
    n_extra = {"bias": 1, "plain": 0, "values_t": 0, "swiglu": 0, "resid": 2, "headnorm": 1, "headrope": 3}[mode]
    extra = refs[1 + n_w:1 + n_w + n_extra]
    o_ref = refs[1 + n_w + n_extra]
    accs = refs[2 + n_w + n_extra:]
    k = pl.program_id(2)

    def epilogue(ys, rows=slice(None)):
        y = ys[0]
        if mode == "bias":
            y = y + extra[0][...]
        elif mode == "swiglu":
            y = _silu(y) * ys[1]
        elif mode == "resid":
            y = extra[0][rows, :] + extra[1][...] * y
        elif mode == "headnorm":
            y = _head_norm_rope(y, extra[0][...], None, None, out_scale)
        elif mode == "headrope":
            y = _head_norm_rope(y, extra[0][...], extra[1][rows, :], extra[2][rows, :], out_scale)
        elif mode == "values_t":
            n_rows = y.shape[0]
            pad = lax.broadcasted_iota(jnp.int32, (BF16_ROWS, n_rows), 0) == 0
            for h in range(y.shape[1] // head_width):
                o_ref[h, :head_width, rows] = y[:, h * head_width:(h + 1) * head_width].T.astype(o_ref.dtype)
                o_ref[h, head_width:, rows] = pad.astype(o_ref.dtype)
            return
        o_ref[rows, :] = y.astype(o_ref.dtype)

    if nk == 1:
        tm = a_ref.shape[0]
        rc = MM_ROW_CHUNK if tm % MM_ROW_CHUNK == 0 else tm
        pending = None
        for r in range(0, tm, rc):
            rows = slice(r, r + rc)
            a = a_ref[rows, :]
            parts = [jnp.dot(a, w[...], preferred_element_type=F32) for w in w_refs]
            if pending is not None:
                epilogue(*pending)
            pending = (parts, rows)
        epilogue(*pending)
        return

    a = a_ref[...]
    parts = [jnp.dot(a, w[...], preferred_element_type=F32) for w in w_refs]

    @pl.when(k == 0)
    def _():
        for acc, p in zip(accs, parts):
            acc[...] = p

    @pl.when(jnp.logical_and(k > 0, k < nk - 1))
    def _():
        for acc, p in zip(accs, parts):
            acc[...] += p

    @pl.when(k == nk - 1)
    def _():
        epilogue([acc[...] + p for acc, p in zip(accs, parts)])


def _matmul(a, ws, mode, out_dtype, *, extra=(), tm=1024, tn=512, tk=None, out_scale=1.0, head_width=None):
    m, kdim = a.shape
    n = ws[0].shape[1]
    tm = _pick(m, tm, BF16_ROWS)
    tn = _pick(n, tn, LANES)
    tk = kdim if tk is None else _pick(kdim, tk, LANES)
    nk = kdim // tk
    grid = (m // tm, n // tn, nk)
    in_specs = [pl.BlockSpec((tm, tk), lambda i, j, k: (i, k))]
    in_specs += [pl.BlockSpec((tk, tn), lambda i, j, k: (k, j)) for _ in ws]
    row = pl.BlockSpec((1, tn), lambda i, j, k: (0, j))
    head = pl.BlockSpec((1, HEAD_DIM), lambda i, j, k: (0, 0))
    ops = []
    if mode == "bias":
        in_specs += [row]
        ops = [extra[0].reshape(1, n)]
    elif mode == "resid":
        in_specs += [pl.BlockSpec((tm, tn), lambda i, j, k: (i, j)), row]
        ops = [extra[0], extra[1].reshape(1, n)]
    elif mode == "headnorm":
        in_specs += [head]
        ops = [extra[0].reshape(1, HEAD_DIM)]
    elif mode == "headrope":
        tab = pl.BlockSpec((tm, HEAD_DIM), lambda i, j, k: (i, 0))
        in_specs += [head, tab, tab]
        ops = [extra[0].reshape(1, HEAD_DIM), extra[1], extra[2]]
    n_acc = len(ws) if nk > 1 else 0
    out_spec = pl.BlockSpec((tm, tn), lambda i, j, k: (i, j))
    out_shape = jax.ShapeDtypeStruct((m, n), out_dtype)
    if mode == "values_t":
        assert nk == 1 and tn % head_width == 0 and tm % LANES == 0
        out_spec = pl.BlockSpec((tn // head_width, head_width + BF16_ROWS, tm), lambda i, j, k: (j, 0, i))
        out_shape = jax.ShapeDtypeStruct((n // head_width, head_width + BF16_ROWS, m), out_dtype)
    return pl.pallas_call(
        functools.partial(_mm_kernel, mode=mode, nk=nk, out_scale=out_scale, head_width=head_width),
        grid=grid,
        in_specs=in_specs,
        out_specs=out_spec,
        out_shape=out_shape,
        scratch_shapes=[pltpu.VMEM((tm, tn), F32) for _ in range(n_acc)],
        compiler_params=_params("parallel", "parallel", "arbitrary"),
        name="mm_" + mode,
    )(a, *ws, *ops)


def _attn_kernel(*refs, kind, n_kv, lam_init):
    if kind == "gqa":
        q_ref, k_ref, vt_ref, o_ref, m_sc, acc_sc = refs
        n_q = GQA_GROUP
    else:
        q_ref, k_ref, vt_ref, lq1, lk1, lq2, lk2, sg_ref, o_ref, m_sc, acc_sc = refs
        n_q = 2
    kv = pl.program_id(2)
    dv = vt_ref.shape[0] - BF16_ROWS

    @pl.when(kv == 0)
    def _():
        m_sc[...] = jnp.full(m_sc.shape, -jnp.inf, F32)
        acc_sc[...] = jnp.zeros(acc_sc.shape, F32)

    vt = vt_ref[...]
    tq = q_ref.shape[0]
    tc = min(tq, ATTN_COL_TILE)
    units = [(h, c) for h in range(n_q) for c in range(tq // tc)]

    def scores_t(u):
        h, c = units[u]
        q = q_ref[c * tc:(c + 1) * tc, h * HEAD_DIM:(h + 1) * HEAD_DIM]
        kh = 0 if kind == "gqa" else h
        k = k_ref[:, kh * HEAD_DIM:(kh + 1) * HEAD_DIM]
        return lax.dot_general(k, q, (((1,), (1,)), ((), ())), preferred_element_type=F32)

    def softmax_t(u, st):
        h, c = units[u]
        m_prev = m_sc[h, :, c * tc:(c + 1) * tc]
        m_new = jnp.maximum(m_prev, jnp.max(st, axis=0, keepdims=True))
        m_sc[h, :, c * tc:(c + 1) * tc] = m_new
        return jnp.exp2(st - m_new).astype(vt.dtype), jnp.exp2(m_prev - m_new)

    def accumulate(u, p, alpha):
        h, c = units[u]
        acc = acc_sc[h, :, c * tc:(c + 1) * tc]
        acc_sc[h, :, c * tc:(c + 1) * tc] = alpha * acc + jnp.dot(vt, p, preferred_element_type=F32)

    st, pa = {}, {}
    lag_b, lag_c = ATTN_SOFTMAX_LAG[kind], ATTN_SOFTMAX_LAG[kind] + ATTN_VALUE_LAG[kind]
    for i in range(len(units) + lag_c):
        if i < len(units):
            st[i] = scores_t(i)
        if 0 <= i - lag_b < len(units):
            pa[i - lag_b] = softmax_t(i - lag_b, st.pop(i - lag_b))
        if 0 <= i - lag_c < len(units):
            accumulate(i - lag_c, *pa.pop(i - lag_c))

    @pl.when(kv == n_kv - 1)
    def _():
        def normalised(h):
            acc = acc_sc[h]
            return acc[:dv] / acc[dv:dv + 1]

        if kind == "gqa":
            for h in range(n_q):
                o_ref[:, h * HEAD_DIM:(h + 1) * HEAD_DIM] = normalised(h).T.astype(o_ref.dtype)
        else:
            lam = (jnp.exp(jnp.sum(lq1[...] * lk1[...], axis=-1, keepdims=True))
                   - jnp.exp(jnp.sum(lq2[...] * lk2[...], axis=-1, keepdims=True)) + lam_init)
            o = (normalised(0) - lam * normalised(1)).T
            ms = jnp.mean(o * o, axis=-1, keepdims=True)
            o = o * lax.rsqrt(ms + NORM_EPS) * sg_ref[...] * (1.0 - lam_init)
            o_ref[...] = o.astype(o_ref.dtype)


def _attention(q, k, vt, kind, *, lam_params=None, sub_gain=None, lam_init=0.0):
    lq, d = q.shape
    t = k.shape[0]
    if kind == "gqa":
        n_heads = d // (GQA_GROUP * HEAD_DIM)
        qw, kw, vw, n_q = GQA_GROUP * HEAD_DIM, HEAD_DIM, HEAD_DIM, GQA_GROUP
    else:
        n_heads = d // (2 * HEAD_DIM)
        qw, kw, vw, n_q = 2 * HEAD_DIM, 2 * HEAD_DIM, 2 * HEAD_DIM, 2
    tq = _pick(lq, 1024, ATTN_COL_TILE)
    tk = _pick(t, ATTN_KEY_TILE[kind], LANES)
    n_kv = t // tk
    vr = vw + BF16_ROWS
    assert vt.shape == (n_heads, vr, t), (vt.shape, n_heads, vr, t)
    in_specs = [
        pl.BlockSpec((tq, qw), lambda h, i, j: (i, h)),
        pl.BlockSpec((tk, kw), lambda h, i, j: (j, h)),
        pl.BlockSpec((None, vr, tk), lambda h, i, j: (h, 0, j)),
    ]
    ops = [q, k, vt]
    if kind == "diff":
        vec = pl.BlockSpec((1, HEAD_DIM), lambda h, i, j: (0, 0))
        in_specs += [vec, vec, vec, vec, pl.BlockSpec((1, 2 * HEAD_DIM), lambda h, i, j: (0, 0))]
        ops += [p.reshape(1, HEAD_DIM) for p in lam_params] + [sub_gain.reshape(1, 2 * HEAD_DIM)]
    return pl.pallas_call(
        functools.partial(_attn_kernel, kind=kind, n_kv=n_kv, lam_init=lam_init),
        grid=(n_heads, lq // tq, n_kv),
        in_specs=in_specs,
        out_specs=pl.BlockSpec((tq, qw), lambda h, i, j: (i, h)),
        out_shape=jax.ShapeDtypeStruct((lq, d), BF16),
        scratch_shapes=[
            pltpu.VMEM((n_q, 1, tq), F32),
            pltpu.VMEM((n_q, vr, tq), F32),
        ],
        compiler_params=_params("parallel", "parallel", "arbitrary"),
        name="attn_" + kind,
    )(*ops)


def _sconv_kernel(z_ref, zp_ref, zn_ref, w_ref, b_ref, o_ref, *, n_row_blocks):
    i = pl.program_id(1)
    z = z_ref[...]
    tr = z.shape[0]
    prev_row = jnp.where(i == 0, 0.0, zp_ref[SUBLANES - 1:SUBLANES, :])
    next_row = jnp.where(i == n_row_blocks - 1, 0.0, zn_ref[0:1, :])
    row = lax.broadcasted_iota(jnp.int32, z.shape, 0)
    z_before = jnp.where(row == 0, prev_row, pltpu.roll(z, 1, axis=0))
    z_after = jnp.where(row == tr - 1, next_row, pltpu.roll(z, tr - 1, axis=0))
    w = w_ref[...]
    o_ref[0] = b_ref[...] + z_before * w[0:1] + z * w[1:2] + z_after * w[2:3]


def _short_conv(z, conv_w, conv_b, n_split):
    l, c = z.shape
    d = c // n_split
    tr = _pick(l, 512, SUBLANES)
    tc = _pick(d, 1024, LANES)
    ncb = d // tc
    nrb = l // tr
    r8 = tr // SUBLANES
    last8 = l // SUBLANES - 1
    return pl.pallas_call(
        functools.partial(_sconv_kernel, n_row_blocks=nrb),
        grid=(n_split, nrb, ncb),
        in_specs=[
            pl.BlockSpec((tr, tc), lambda s, i, j: (i, s * ncb + j)),
            pl.BlockSpec((SUBLANES, tc), lambda s, i, j: (jnp.maximum(i * r8 - 1, 0), s * ncb + j)),
            pl.BlockSpec((SUBLANES, tc), lambda s, i, j: (jnp.minimum((i + 1) * r8, last8), s * ncb + j)),
            pl.BlockSpec((conv_w.shape[0], tc), lambda s, i, j: (0, s * ncb + j)),
            pl.BlockSpec((1, tc), lambda s, i, j: (0, s * ncb + j)),
        ],
        out_specs=pl.BlockSpec((1, tr, tc), lambda s, i, j: (s, i, j)),
        out_shape=jax.ShapeDtypeStruct((n_split, l, d), F32),
        compiler_params=_params("parallel", "parallel", "parallel"),
        name="short_conv",
    )(z, z, z, conv_w, conv_b.reshape(1, c))


def _filter_mlp_kernel(f_ref, w1_ref, b1_ref, w2_ref, b2_ref, fr_ref, o_ref):
    g = jnp.dot(f_ref[...], w1_ref[...], precision=HIGHEST, preferred_element_type=F32)
    g = jnp.sin(fr_ref[0:1] * (g + b1_ref[...]))
    g = jnp.dot(g, w2_ref[...], precision=HIGHEST, preferred_element_type=F32)
    o_ref[...] = jnp.sin(fr_ref[1:2] * (g + b2_ref[...]))


def _filter_taps_fft1_kernel(g_ref, w3f_ref, w3b_ref, dl_ref, f_ref, tc_ref, ts_ref,
                             are_ref, aim_ref, asum_ref, *, n_tok, n2_size, k1p):
    n2 = pl.program_id(1)
    n1_size = g_ref.shape[0]
    half = n1_size // 2
    g = g_ref[...]
    taps = jnp.concatenate([
        jnp.dot(g[:half], w3f_ref[...], precision=HIGHEST, preferred_element_type=F32),
        jnp.dot(g[half:], w3b_ref[...], precision=HIGHEST, preferred_element_type=F32)], axis=0)
    n = lax.broadcasted_iota(jnp.int32, (n1_size, 1), 0) * n2_size + n2
    tau = jnp.where(n < n_tok, n, 2 * n_tok - n)
    t = tau.astype(F32) * (1.0 / (n_tok - 1))
    taps = taps * jnp.exp(-t * dl_ref[...])
    taps = jnp.where(n == n_tok, 0.0, taps)

    @pl.when(n2 == 0)
    def _():
        asum_ref[...] = jnp.zeros(asum_ref.shape, F32)

    asum_ref[...] += jnp.sum(jnp.abs(taps), axis=0, keepdims=True)

    a = jnp.dot(f_ref[...], _split3(taps), preferred_element_type=F32)
    re, im = a[:k1p], a[k1p:]
    c = _table_column(tc_ref, n2)
    s = _table_column(ts_ref, n2)
    are_ref[...] = re * c + im * s
    aim_ref[...] = im * c - re * s


def _hyena_filter_stage1(n_tok, d, consts, f_w1, f_b1, f_w2, f_b2, f_w3, f_freq):
    width = f_w1.shape[1]
    t = jnp.linspace(0.0, 1.0, n_tok, dtype=F32)[:, None]
    bands = (HY_EMB_DIM - 1) // 2
    w = (2.0 * math.pi / n_tok) * jnp.arange(n_tok, dtype=F32)[:, None]
    f = jnp.linspace(1e-4, bands - 1, bands, dtype=F32)[None, :]
    feats = jnp.concatenate([t, jnp.cos(f * w), -jnp.sin(f * w)], axis=-1)
    feats2 = jnp.concatenate([feats, feats[:1], feats[:0:-1]], axis=0)
    feats2 = jnp.pad(feats2, ((0, 0), (0, LANES - HY_EMB_DIM)))
    pw = LANES - width
    w1 = jnp.pad(f_w1, ((0, LANES - HY_EMB_DIM), (0, pw)))
    w2 = jnp.pad(f_w2, ((0, pw), (0, pw)))
    b1 = jnp.pad(f_b1, (0, pw)).reshape(1, LANES)
    b2 = jnp.pad(f_b2, (0, pw)).reshape(1, LANES)
    freq = jnp.pad(f_freq, ((0, 0), (0, pw)))
    n2l = 2 * n_tok
    tr = _pick(n_tok, 512, SUBLANES)
    full = lambda shape: pl.BlockSpec(shape, lambda i: (0,) * len(shape))
    g = pl.pallas_call(
        _filter_mlp_kernel,
        grid=(n2l // tr,),
        in_specs=[pl.BlockSpec((tr, LANES), lambda i: (i, 0)), full((LANES, LANES)), full((1, LANES)),
                  full((LANES, LANES)), full((1, LANES)), full((2, LANES))],
        out_specs=pl.BlockSpec((tr, LANES), lambda i: (i, 0)),
        out_shape=jax.ShapeDtypeStruct((n2l, LANES), F32),
        compiler_params=_params("parallel"),
        name="hyena_filter_mlp",
    )(feats2, w1, b1, w2, b2, freq)

    c = HY_ORDER * d
    w3 = jnp.pad(f_w3, ((0, pw), (0, 0)))
    max_decay = math.log(HY_DECAY_TARGET) / HY_FAST_DECAY_PCT
    min_decay = math.log(HY_DECAY_TARGET) / HY_SLOW_DECAY_PCT
    deltas = jnp.abs(jnp.linspace(min_decay, max_decay, d, dtype=F32))
    deltas = jnp.tile(deltas, HY_ORDER).reshape(1, c)
    _, n1, n2, _, k1p = _dft_plan(n_tok)
    g_by_n2 = g.reshape(n1, n2, LANES).transpose(1, 0, 2)
    tcol = _pick(c, 2048, LANES)
    ncb = c // tcol
    tab = pl.BlockSpec((k1p, n2), lambda j, i: (0, 0))
    out = pl.BlockSpec((k1p, tcol), lambda j, i: (0, i * ncb + j))
    return pl.pallas_call(
        functools.partial(_filter_taps_fft1_kernel, n_tok=n_tok, n2_size=n2, k1p=k1p),
        grid=(ncb, n2),
        in_specs=[
            pl.BlockSpec((None, n1, LANES), lambda j, i: (i, 0, 0)),
            pl.BlockSpec((LANES, tcol), lambda j, i: (0, j)),
            pl.BlockSpec((LANES, tcol), lambda j, i: (0, ncb + j)),
            pl.BlockSpec((1, tcol), lambda j, i: (0, j)),
            pl.BlockSpec((2 * k1p, 3 * n1), lambda j, i: (0, 0)), tab, tab,
        ],
        out_specs=[out, out, pl.BlockSpec((1, tcol), lambda j, i: (0, j))],
        out_shape=[jax.ShapeDtypeStruct((k1p, n2 * c), F32)] * 2 + [jax.ShapeDtypeStruct((1, c), F32)],
        compiler_params=_params("parallel", "arbitrary"),
        name="hyena_filter_fft1",
    )(g_by_n2, w3, w3, deltas, consts["f1_full"], consts["tw_cos"], consts["tw_sin"])


def _split3(x):
    hi = x.astype(BF16)
    lo = (x - hi.astype(F32)).astype(BF16)
    return jnp.concatenate([hi, lo, hi], axis=0)


def _mat3(m64):
    m = jnp.asarray(m64, F32)
    hi = m.astype(BF16)
    lo = (m - hi.astype(F32)).astype(BF16)
    return jnp.concatenate([hi, hi, lo], axis=1)


def _dft_plan(n_tok):
    n = 2 * n_tok
    n1 = 1 << ((n.bit_length() - 1 + 1) // 2)
    n2 = n // n1
    assert n1 * n2 == n and n2 >= SUBLANES, (n_tok, n1, n2)
    k1 = n1 // 2 + 1
    k1p = -(-k1 // BF16_ROWS) * BF16_ROWS
    return n, n1, n2, k1, k1p


def _dft_constants(n_tok):
    n, n1, n2, k1, k1p = _dft_plan(n_tok)
    two_pi = 2.0 * np.pi
    kk = np.arange(k1p)[:, None]
    valid = (kk < k1).astype(np.float64)
    ang = two_pi * ((kk * np.arange(n1)[None, :]) % n1) / n1
    f1 = np.concatenate([np.cos(ang) * valid, -np.sin(ang) * valid], axis=0)
    a2 = two_pi * ((np.arange(n2)[:, None] * np.arange(n2)[None, :]) % n2) / n2
    c2, s2 = np.cos(a2), np.sin(a2)
    f2 = np.block([[c2, s2], [-s2, c2]])
    g2 = np.block([[c2, -s2], [s2, c2]])
    wk = np.where((np.arange(k1p) == 0) | (np.arange(k1p) == n1 // 2), 1.0, 2.0) * (np.arange(k1p) < k1)
    a3 = two_pi * ((np.arange(n1 // 2)[:, None] * np.arange(k1p)[None, :]) % n1) / n1
    g3 = np.concatenate([np.cos(a3) * wk[None, :], -np.sin(a3) * wk[None, :]], axis=1) / n
    at = two_pi * (kk * np.arange(n2)[None, :]) / n
    return dict(
        f1_full=_mat3(f1), f1_half=_mat3(f1[:, :n1 // 2]), f2=_mat3(f2), g2=_mat3(g2), g3=_mat3(g3),
        tw_cos=jnp.asarray(np.cos(at), F32), tw_sin=jnp.asarray(np.sin(at), F32),
        tw_cos_t=jnp.asarray(np.cos(at).T, F32), tw_sin_t=jnp.asarray(np.sin(at).T, F32),
    )


def _table_column(tab_ref, idx):
    lane = lax.broadcasted_iota(jnp.int32, tab_ref.shape, 1)
    return jnp.sum(jnp.where(lane == idx, tab_ref[...], 0.0), axis=1, keepdims=True)


def _fft1_kernel(x_ref, f_ref, tc_ref, ts_ref, are_ref, aim_ref, *, k1p, blocks_per_n2):
    n2 = pl.program_id(0) // blocks_per_n2
    a = jnp.dot(f_ref[...], _split3(x_ref[...]), preferred_element_type=F32)
    re, im = a[:k1p], a[k1p:]
    c = _table_column(tc_ref, n2)
    s = _table_column(ts_ref, n2)
    are_ref[...] = re * c + im * s
    aim_ref[...] = im * c - re * s


def _fft_stage1(x2d, f1, consts, n_tok, n_ch):
    _, n1, n2, _, k1p = _dft_plan(n_tok)
    n1c, cols = x2d.shape
    tcol = _pick(n_ch, 2048, LANES)
    tab = pl.BlockSpec((k1p, n2), lambda j: (0, 0))
    out = pl.BlockSpec((k1p, tcol), lambda j: (0, j))
    return pl.pallas_call(
        functools.partial(_fft1_kernel, k1p=k1p, blocks_per_n2=n_ch // tcol),
        grid=(cols // tcol,),
        in_specs=[pl.BlockSpec((n1c, tcol), lambda j: (0, j)),
                  pl.BlockSpec((2 * k1p, 3 * n1c), lambda j: (0, 0)), tab, tab],
        out_specs=[out, out],
        out_shape=[jax.ShapeDtypeStruct((k1p, cols), F32)] * 2,
        compiler_params=_params("parallel"),
        name="hyena_fft1",
    )(x2d, f1, consts["tw_cos"], consts["tw_sin"])


def _fft2_spec_kernel(are_ref, aim_ref, f_ref, asum_ref, sre_ref, sim_ref, *, tk1, n2):
    inv = 1.0 / (asum_ref[...] + NORM_EPS)

    def forward(r):
        a = jnp.concatenate([are_ref[r], aim_ref[r]], axis=0)
        return jnp.dot(f_ref[...], _split3(a), preferred_element_type=F32)

    def store(r, x):
        sre_ref[r] = x[:n2] * inv
        sim_ref[r] = x[n2:] * inv

    xs = {}
    for r in range(tk1 + DFT_ROW_LAG):
        if r < tk1:
            xs[r] = forward(r)
        if r >= DFT_ROW_LAG:
            store(r - DFT_ROW_LAG, xs.pop(r - DFT_ROW_LAG))


def _fft2_conv_kernel(are_ref, aim_ref, sre_ref, sim_ref, f_ref, g_ref, tc_ref, ts_ref,
                      bre_ref, bim_ref, *, tk1, n2):
    i = pl.program_id(0)

    def forward(r):
        a = jnp.concatenate([are_ref[r], aim_ref[r]], axis=0)
        return jnp.dot(f_ref[...], _split3(a), preferred_element_type=F32)

    def inverse(r, x):
        xr, xi = x[:n2], x[n2:]
        sr, si = sre_ref[r], sim_ref[r]
        y = jnp.concatenate([xr * sr - xi * si, xr * si + xi * sr], axis=0)
        return jnp.dot(g_ref[...], _split3(y), preferred_element_type=F32)

    def store(r, b):
        br, bi = b[:n2], b[n2:]
        c = _table_column(tc_ref, i * tk1 + r)
        s = _table_column(ts_ref, i * tk1 + r)
        bre_ref[r] = br * c - bi * s
        bim_ref[r] = bi * c + br * s

    xs, bs = {}, {}
    for r in range(tk1 + 2 * DFT_ROW_LAG):
        if r < tk1:
            xs[r] = forward(r)
        if 0 <= r - DFT_ROW_LAG < tk1:
            bs[r - DFT_ROW_LAG] = inverse(r - DFT_ROW_LAG, xs.pop(r - DFT_ROW_LAG))
        if 0 <= r - 2 * DFT_ROW_LAG < tk1:
            store(r - 2 * DFT_ROW_LAG, bs.pop(r - 2 * DFT_ROW_LAG))


def _fft_stage2(are, aim, consts, n_tok, n_ch, *, spec=None, asum=None, spec_offset=0):
    _, n1, n2, _, k1p = _dft_plan(n_tok)
    tk1 = SUBLANES
    td = _pick(n_ch, 512, LANES)
    blk = pl.BlockSpec((tk1, n2, td), lambda i, j: (i, 0, j))
    mat = pl.BlockSpec((2 * n2, 6 * n2), lambda i, j: (0, 0))
    grid = (k1p // tk1, n_ch // td)
    out_shape = [jax.ShapeDtypeStruct((k1p, n2, n_ch), F32)] * 2
    if spec is None:
        return pl.pallas_call(
            functools.partial(_fft2_spec_kernel, tk1=tk1, n2=n2),
            grid=grid,
            in_specs=[blk, blk, mat, pl.BlockSpec((1, td), lambda i, j: (0, j))],
            out_specs=[blk, blk],
            out_shape=out_shape,
            compiler_params=_params("parallel", "parallel"),
            name="hyena_fft2_spec",
        )(are, aim, consts["f2"], asum)
    off = spec_offset // td
    sblk = pl.BlockSpec((tk1, n2, td), lambda i, j: (i, 0, j + off))
    tab = pl.BlockSpec((n2, k1p), lambda i, j: (0, 0))
    return pl.pallas_call(
        functools.partial(_fft2_conv_kernel, tk1=tk1, n2=n2),
        grid=grid,
        in_specs=[blk, blk, sblk, sblk, mat, mat, tab, tab],
        out_specs=[blk, blk],
        out_shape=out_shape,
        compiler_params=_params("parallel", "parallel"),
        name="hyena_fft2_conv",
    )(are, aim, spec[0], spec[1], consts["f2"], consts["g2"], consts["tw_cos_t"], consts["tw_sin_t"])


def _fft3_kernel(bre_ref, bim_ref, g_ref, y_ref, gate_ref, bias_ref, o_ref):
    b = jnp.concatenate([bre_ref[...], bim_ref[...]], axis=0)
    conv = jnp.dot(g_ref[...], _split3(b), preferred_element_type=F32)
    o_ref[...] = (gate_ref[...] * (conv + y_ref[...] * bias_ref[...])).astype(o_ref.dtype)


def _fft_stage3(bre, bim, consts, n_tok, n_ch, y2d, gate2d, bias, out_dtype):
    _, n1, n2, _, k1p = _dft_plan(n_tok)
    cols = n2 * n_ch
    tcol = _pick(n_ch, 2048, LANES)
    per = n_ch // tcol
    inb = pl.BlockSpec((k1p, tcol), lambda j: (0, j))
    sig = pl.BlockSpec((n1 // 2, tcol), lambda j: (0, j))
    return pl.pallas_call(
        _fft3_kernel,
        grid=(cols // tcol,),
        in_specs=[inb, inb, pl.BlockSpec((n1 // 2, 6 * k1p), lambda j: (0, 0)), sig, sig,
                  pl.BlockSpec((1, tcol), lambda j: (0, j % per))],
        out_specs=sig,
        out_shape=jax.ShapeDtypeStruct((n1 // 2, cols), out_dtype),
        compiler_params=_params("parallel"),
        name="hyena_fft3",
    )(bre, bim, consts["g3"], y2d, gate2d, bias.reshape(1, n_ch))


def _hyena_mixer_core(h, w_in, b_in, conv_w, conv_b, f_w1, f_b1, f_w2, f_b2, f_w3, f_freq, f_bias):
    n_tok, d = h.shape
    _, n1, n2, _, k1p = _dft_plan(n_tok)
    consts = _dft_constants(n_tok)
    z = _matmul(h, (w_in,), "bias", F32, extra=(b_in,))
    zs = _short_conv(z, conv_w, conv_b, HY_ORDER + 1)

    c = HY_ORDER * d
    tre, tim, asum = _hyena_filter_stage1(n_tok, d, consts, f_w1, f_b1, f_w2, f_b2, f_w3, f_freq)
    spec = _fft_stage2(tre.reshape(k1p, n2, c), tim.reshape(k1p, n2, c), consts, n_tok, c, asum=asum)

    y = zs[0].reshape(n1 // 2, n2 * d)
    for n in range(HY_ORDER):
        are, aim = _fft_stage1(y, consts["f1_half"], consts, n_tok, d)
        bre, bim = _fft_stage2(are.reshape(k1p, n2, d), aim.reshape(k1p, n2, d), consts, n_tok, d,
                               spec=spec, spec_offset=n * d)
        last = n == HY_ORDER - 1
        y = _fft_stage3(bre.reshape(k1p, n2 * d), bim.reshape(k1p, n2 * d), consts, n_tok, d, y,
                        zs[n + 1].reshape(n1 // 2, n2 * d), f_bias[n], BF16 if last else F32)
    return y.reshape(n_tok, d)


def _axial_rope_tables(n_tok):
    rows = n_tok // GRID_W
    row = jnp.repeat(jnp.arange(rows, dtype=F32), GRID_W)
    col = (jnp.arange(rows * GRID_W) % GRID_W).astype(F32)
    n_pair = HEAD_DIM // 4
    inv_freq = ROPE_THETA ** (-jnp.arange(n_pair, dtype=F32) / n_pair)
    ang = jnp.concatenate([row[:, None] * inv_freq, col[:, None] * inv_freq], axis=-1)
    cos, sin = jnp.cos(ang), jnp.sin(ang)
    signed_sin = jnp.stack([-sin, sin], axis=-1).reshape(sin.shape[0], HEAD_DIM)
    return jnp.repeat(cos, 2, axis=-1), signed_sin


def kernel(x, c, ctx, c_ctx, ada_a, ada_b, ada_bias, norm1_g, norm2_g, ffn_w_gate, ffn_w_up, ffn_w_down,
           hy_w_in, hy_b_in, hy_conv_w, hy_conv_b, hy_f_w1, hy_f_b1, hy_f_w2, hy_f_b2, hy_f_w3, hy_f_freq,
           hy_f_bias, hy_w_out, gqa_w_q, gqa_w_k, gqa_w_v, gqa_q_norm, gqa_k_norm, gqa_w_o,
           diff_w_q, diff_w_k, diff_w_v, diff_q_norm, diff_k_norm, diff_lambda_q1, diff_lambda_k1,
           diff_lambda_q2, diff_lambda_k2, diff_sub_norm, diff_w_o):
    assert x.shape[0] == 1 and ctx.shape[0] == 1
    depth = ada_a.shape[0]
    n_tok, d = x.shape[1], x.shape[2]
    xs = x[0]
    cs = ctx[0]
    cos, sin = _axial_rope_tables(n_tok)
    n_ctx = ctx.shape[1]
    cos_all = jnp.concatenate([cos, jnp.ones((n_ctx, HEAD_DIM), F32)], axis=0)
    sin_all = jnp.concatenate([sin, jnp.zeros((n_ctx, HEAD_DIM), F32)], axis=0)

    def keys_values(h, hc, w_k, w_v, k_gain, head_width):
        h_all = jnp.concatenate([h, hc], axis=0)
        t = h_all.shape[0]
        tm = _pick(t, 1280, MM_ROW_CHUNK) if t % MM_ROW_CHUNK == 0 else t
        k = _matmul(h_all, (w_k,), "headrope", BF16, extra=(k_gain, cos_all, sin_all), tm=tm)
        vt = _matmul(h_all, (w_v,), "values_t", BF16, tm=tm, head_width=head_width)
        return k, vt

    cc = jnp.concatenate([c.reshape(1, d), c_ctx.reshape(1, d), jnp.zeros((SUBLANES - 2, d), F32)], axis=0)
    mods = _ada_mods(cc, ada_a, ada_b, ada_bias)

    kinds = [i % N_MIXERS for i in range(depth)]
    for i in range(depth):
        kind = kinds[i]
        j = i // N_MIXERS
        reads_ctx = kind != MIX_HYENA
        update_ctx = any(kk != MIX_HYENA for kk in kinds[i + 1:])
        sh1, sc1, g1, sh2, sc2, g2 = [mods[i, 0, m * d:(m + 1) * d] for m in range(N_MOD)]
        csh1, csc1, cg1, csh2, csc2, cg2 = [mods[i, 1, m * d:(m + 1) * d] for m in range(N_MOD)]
        h = _norm_mod(xs, norm1_g[i], sh1, sc1)
        hc = _norm_mod(cs, norm1_g[i], csh1, csc1) if (reads_ctx or update_ctx) else None
        yc = None
        if kind == MIX_HYENA:
            w_in = hy_w_in[j].astype(BF16)
            w_out = hy_w_out[j].astype(BF16)
            args = (w_in, hy_b_in[j], hy_conv_w[j], hy_conv_b[j], hy_f_w1[j], hy_f_b1[j], hy_f_w2[j],
                    hy_f_b2[j], hy_f_w3[j], hy_f_freq[j], hy_f_bias[j])
            y = _hyena_mixer_core(h, *args)
            xs = _matmul(y, (w_out,), "resid", F32, extra=(xs, g1))
            if update_ctx:
                yc = _hyena_mixer_core(hc, *args)
                cs = _matmul(yc, (w_out,), "resid", F32, extra=(cs, cg1))
        elif kind == MIX_GQA:
            w_q = gqa_w_q[j].astype(BF16)
            w_k = gqa_w_k[j].astype(BF16)
            w_v = gqa_w_v[j].astype(BF16)
            w_o = gqa_w_o[j].astype(BF16)
            qg, kg = gqa_q_norm[j], gqa_k_norm[j]
            q = _matmul(h, (w_q,), "headrope", BF16, extra=(qg, cos, sin), out_scale=ATTN_Q_SCALE)
            k, vt = keys_values(h, hc, w_k, w_v, kg, HEAD_DIM)
            o = _attention(q, k, vt, "gqa")
            xs = _matmul(o, (w_o,), "resid", F32, extra=(xs, g1))
            if update_ctx:
                qc = _matmul(hc, (w_q,), "headnorm", BF16, extra=(qg,), out_scale=ATTN_Q_SCALE)
                oc = _attention(qc, k[n_tok:], vt[:, :, n_tok:], "gqa")
                cs = _matmul(oc, (w_o,), "resid", F32, extra=(cs, cg1))
        else:
            w_q = diff_w_q[j].astype(BF16)
            w_k = diff_w_k[j].astype(BF16)
            w_v = diff_w_v[j].astype(BF16)
            w_o = diff_w_o[j].astype(BF16)
            qg, kg = diff_q_norm[j], diff_k_norm[j]
            lam_init = 0.8 - 0.6 * math.exp(-0.3 * i)
            lam_params = (diff_lambda_q1[j], diff_lambda_k1[j], diff_lambda_q2[j], diff_lambda_k2[j])
            attend = functools.partial(_attention, kind="diff", lam_params=lam_params,
                                       sub_gain=diff_sub_norm[j], lam_init=lam_init)
            q = _matmul(h, (w_q,), "headrope", BF16, extra=(qg, cos, sin), out_scale=ATTN_Q_SCALE)
            k, vt = keys_values(h, hc, w_k, w_v, kg, 2 * HEAD_DIM)
            o = attend(q, k, vt)
            xs = _matmul(o, (w_o,), "resid", F32, extra=(xs, g1))
            if update_ctx:
                qc = _matmul(hc, (w_q,), "headnorm", BF16, extra=(qg,), out_scale=ATTN_Q_SCALE)
                oc = attend(qc, k[n_tok:], vt[:, :, n_tok:])
                cs = _matmul(oc, (w_o,), "resid", F32, extra=(cs, cg1))

        w_gate = ffn_w_gate[i].astype(BF16)
        w_up = ffn_w_up[i].astype(BF16)
        w_down = ffn_w_down[i].astype(BF16)
        h2 = _norm_mod(xs, norm2_g[i], sh2, sc2)
        act = _matmul(h2, (w_gate, w_up), "swiglu", BF16, tn=256)
        xs = _matmul(act, (w_down,), "resid", F32, extra=(xs, g2), tm=512, tn=256)
        if update_ctx:
            hc2 = _norm_mod(cs, norm2_g[i], csh2, csc2)
            actc = _matmul(hc2, (w_gate, w_up), "swiglu", BF16, tn=256)
            cs = _matmul(actc, (w_down,), "resid", F32, extra=(cs, cg2), tm=512, tn=256)
    return xs[None]
```

```python
import functools
import math

import numpy as np
import jax
import jax.numpy as jnp
from jax import lax
from jax.experimental import pallas as pl
from jax.experimental.pallas import tpu as pltpu

HEAD_DIM = 128
GRID_W = 64
GQA_GROUP = 4
N_MOD = 6
N_MIXERS = 3
MIX_HYENA, MIX_GQA, MIX_DIFF = 0, 1, 2
NORM_EPS = 1e-6
ROPE_THETA = 10000.0
HY_ORDER = 2
HY_EMB_DIM = 33
HY_DECAY_TARGET = 1e-2
HY_FAST_DECAY_PCT = 0.3
HY_SLOW_DECAY_PCT = 1.5
ATTN_Q_SCALE = HEAD_DIM ** -0.5 * math.log2(math.e)
V7X_MXU_WIDTH = 256
ATTN_COL_TILE = V7X_MXU_WIDTH
ATTN_KEY_TILE = {"gqa": 1280, "diff": 3328}
ATTN_SOFTMAX_LAG = {"gqa": 3, "diff": 2}
ATTN_VALUE_LAG = {"gqa": 2, "diff": 1}
V7X_VMEM_BYTES = 64 * 1024 * 1024
VMEM_LIMIT_BYTES = V7X_VMEM_BYTES - 8 * 1024 * 1024
LANES = 128
SUBLANES = 8
BF16_ROWS = 16
DFT_ROW_LAG = 1
MM_ROW_CHUNK = 256

F32 = jnp.float32
BF16 = jnp.bfloat16
HIGHEST = lax.Precision.HIGHEST


def _pick(dim, pref, align):
    if dim <= pref:
        return dim
    best = None
    for d in range(align, pref + 1, align):
        if dim % d == 0:
            best = d
    assert best is not None, (dim, pref, align)
    return best


def _params(*sem):
    return pltpu.CompilerParams(dimension_semantics=sem, vmem_limit_bytes=VMEM_LIMIT_BYTES)


def _silu(x):
    return x * (1.0 / (1.0 + jnp.exp(-x)))


def _ada_kernel(cc_ref, a_ref, b_ref, bias_ref, o_ref):
    s = _silu(cc_ref[...])
    t = jnp.dot(s, a_ref[0], precision=HIGHEST, preferred_element_type=F32)
    o_ref[0] = jnp.dot(t, b_ref[0], precision=HIGHEST, preferred_element_type=F32) + bias_ref[0]


def _ada_mods(cc, ada_a, ada_b, ada_bias):
    depth, d, r = ada_a.shape
    n = ada_b.shape[2]
    tn = _pick(n, 2048, LANES)
    return pl.pallas_call(
        _ada_kernel,
        grid=(depth, n // tn),
        in_specs=[
            pl.BlockSpec((SUBLANES, d), lambda l, j: (0, 0)),
            pl.BlockSpec((1, d, r), lambda l, j: (l, 0, 0)),
            pl.BlockSpec((1, r, tn), lambda l, j: (l, 0, j)),
            pl.BlockSpec((1, 1, tn), lambda l, j: (l, 0, j)),
        ],
        out_specs=pl.BlockSpec((1, SUBLANES, tn), lambda l, j: (l, 0, j)),
        out_shape=jax.ShapeDtypeStruct((depth, SUBLANES, n), F32),
        compiler_params=_params("parallel", "parallel"),
        name="ada_mods",
    )(cc, ada_a, ada_b, ada_bias.reshape(depth, 1, n))


def _norm_mod_kernel(x_ref, g_ref, sh_ref, sc_ref, o_ref):
    x = x_ref[...]
    ms = jnp.mean(x * x, axis=-1, keepdims=True)
    y = x * lax.rsqrt(ms + NORM_EPS) * g_ref[...]
    o_ref[...] = (y * (1.0 + sc_ref[...]) + sh_ref[...]).astype(o_ref.dtype)


def _norm_mod(x, gain, shift, scale):
    m, d = x.shape
    tr = _pick(m, 256, SUBLANES)
    vec = pl.BlockSpec((1, d), lambda i: (0, 0))
    return pl.pallas_call(
        _norm_mod_kernel,
        grid=(m // tr,),
        in_specs=[pl.BlockSpec((tr, d), lambda i: (i, 0)), vec, vec, vec],
        out_specs=pl.BlockSpec((tr, d), lambda i: (i, 0)),
        out_shape=jax.ShapeDtypeStruct((m, d), BF16),
        compiler_params=_params("parallel"),
        name="norm_mod",
    )(x, gain.reshape(1, d), shift.reshape(1, d), scale.reshape(1, d))


def _head_norm_rope(y, gain, cos, sin, out_scale):
    outs = []
    if cos is not None:
        even = lax.broadcasted_iota(jnp.int32, cos.shape, 1) % 2 == 0
    for h in range(y.shape[1] // HEAD_DIM):
        t = y[:, h * HEAD_DIM:(h + 1) * HEAD_DIM]
        ms = jnp.mean(t * t, axis=-1, keepdims=True)
        t = t * lax.rsqrt(ms + NORM_EPS) * gain
        if cos is not None:
            partner = jnp.where(even, pltpu.roll(t, HEAD_DIM - 1, axis=1), pltpu.roll(t, 1, axis=1))
            t = t * cos + partner * sin
        if out_scale != 1.0:
            t = t * out_scale
        outs.append(t)
    return outs[0] if len(outs) == 1 else jnp.concatenate(outs, axis=1)


def _mm_kernel(*refs, mode, nk, out_scale, head_width):
    n_w = 2 if mode == "swiglu" else 1
    a_ref, w_refs = refs[0], refs[1:1 + n_w]
    n_extra = {"bias": 1, "plain": 0, "values_t": 0, "swiglu": 0, "resid": 2, "headnorm": 1, "headrope": 3}[mode]
    extra = refs[1 + n_w:1 + n_w + n_extra]
    o_ref = refs[1 + n_w + n_extra]
    accs = refs[2 + n_w + n_extra:]
    k = pl.program_id(2)

    def epilogue(ys, rows=slice(None)):
        y = ys[0]
        if mode == "bias":
            y = y + extra[0][...]
        elif mode == "swiglu":
            y = _silu(y) * ys[1]
        elif mode == "resid":
            y = extra[0][rows, :] + extra[1][...] * y
        elif mode == "headnorm":
            y = _head_norm_rope(y, extra[0][...], None, None, out_scale)
        elif mode == "headrope":
            y = _head_norm_rope(y, extra[0][...], extra[1][rows, :], extra[2][rows, :], out_scale)
        elif mode == "values_t":
            n_rows = y.shape[0]
            pad = lax.broadcasted_iota(jnp.int32, (BF16_ROWS, n_rows), 0) == 0
            for h in range(y.shape[1] // head_width):
                o_ref[h, :head_width, rows] = y[:, h * head_width:(h + 1) * head_width].T.astype(o_ref.dtype)
                o_ref[h, head_width:, rows] = pad.astype(o_ref.dtype)
            return
        o_ref[rows, :] = y.astype(o_ref.dtype)

    if nk == 1:
        tm = a_ref.shape[0]
        rc = MM_ROW_CHUNK if tm % MM_ROW_CHUNK == 0 else tm
        pending = None
        for r in range(0, tm, rc):
            rows = slice(r, r + rc)
            a = a_ref[rows, :]
            parts = [jnp.dot(a, w[...], preferred_element_type=F32) for w in w_refs]
            if pending is not None:
                epilogue(*pending)
            pending = (parts, rows)
        epilogue(*pending)
        return

    a = a_ref[...]
    parts = [jnp.dot(a, w[...], preferred_element_type=F32) for w in w_refs]

    @pl.when(k == 0)
    def _():
        for acc, p in zip(accs, parts):
            acc[...] = p

    @pl.when(jnp.logical_and(k > 0, k < nk - 1))
    def _():
        for acc, p in zip(accs, parts):
            acc[...] += p

    @pl.when(k == nk - 1)
    def _():
        epilogue([acc[...] + p for acc, p in zip(accs, parts)])


def _matmul(a, ws, mode, out_dtype, *, extra=(), tm=1024, tn=512, tk=None, out_scale=1.0, head_width=None):
    m, kdim = a.shape
    n = ws[0].shape[1]
    tm = _pick(m, tm, BF16_ROWS)
    tn = _pick(n, tn, LANES)
    tk = kdim if tk is None else _pick(kdim, tk, LANES)
    nk = kdim // tk
    grid = (m // tm, n // tn, nk)
    in_specs = [pl.BlockSpec((tm, tk), lambda i, j, k: (i, k))]
    in_specs += [pl.BlockSpec((tk, tn), lambda i, j, k: (k, j)) for _ in ws]
    row = pl.BlockSpec((1, tn), lambda i, j, k: (0, j))
    head = pl.BlockSpec((1, HEAD_DIM), lambda i, j, k: (0, 0))
    ops = []
    if mode == "bias":
        in_specs += [row]
        ops = [extra[0].reshape(1, n)]
    elif mode == "resid":
        in_specs += [pl.BlockSpec((tm, tn), lambda i, j, k: (i, j)), row]
        ops = [extra[0], extra[1].reshape(1, n)]
    elif mode == "headnorm":
        in_specs += [head]
        ops = [extra[0].reshape(1, HEAD_DIM)]
    elif mode == "headrope":
        tab = pl.BlockSpec((tm, HEAD_DIM), lambda i, j, k: (i, 0))
        in_specs += [head, tab, tab]
        ops = [extra[0].reshape(1, HEAD_DIM), extra[1], extra[2]]
    n_acc = len(ws) if nk > 1 else 0
    out_spec = pl.BlockSpec((tm, tn), lambda i, j, k: (i, j))
    out_shape = jax.ShapeDtypeStruct((m, n), out_dtype)
    if mode == "values_t":
        assert nk == 1 and tn % head_width == 0 and tm % LANES == 0
        out_spec = pl.BlockSpec((tn // head_width, head_width + BF16_ROWS, tm), lambda i, j, k: (j, 0, i))
        out_shape = jax.ShapeDtypeStruct((n // head_width, head_width + BF16_ROWS, m), out_dtype)
    return pl.pallas_call(
        functools.partial(_mm_kernel, mode=mode, nk=nk, out_scale=out_scale, head_width=head_width),
        grid=grid,
        in_specs=in_specs,
        out_specs=out_spec,
        out_shape=out_shape,
        scratch_shapes=[pltpu.VMEM((tm, tn), F32) for _ in range(n_acc)],
        compiler_params=_params("parallel", "parallel", "arbitrary"),
        name="mm_" + mode,
    )(a, *ws, *ops)


def _attn_kernel(*refs, kind, n_kv, lam_init):
    if kind == "gqa":
        q_ref, k_ref, vt_ref, o_ref, m_sc, acc_sc = refs
        n_q = GQA_GROUP
    else:
        q_ref, k_ref, vt_ref, lq1, lk1, lq2, lk2, sg_ref, o_ref, m_sc, acc_sc = refs
        n_q = 2
    kv = pl.program_id(2)
    dv = vt_ref.shape[0] - BF16_ROWS

    @pl.when(kv == 0)
    def _():
        m_sc[...] = jnp.full(m_sc.shape, -jnp.inf, F32)
        acc_sc[...] = jnp.zeros(acc_sc.shape, F32)

    vt = vt_ref[...]
    tq = q_ref.shape[0]
    tc = min(tq, ATTN_COL_TILE)
    units = [(h, c) for h in range(n_q) for c in range(tq // tc)]

    def scores_t(u):
        h, c = units[u]
        q = q_ref[c * tc:(c + 1) * tc, h * HEAD_DIM:(h + 1) * HEAD_DIM]
        kh = 0 if kind == "gqa" else h
        k = k_ref[:, kh * HEAD_DIM:(kh + 1) * HEAD_DIM]
        return lax.dot_general(k, q, (((1,), (1,)), ((), ())), preferred_element_type=F32)

    def softmax_t(u, st):
        h, c = units[u]
        m_prev = m_sc[h, :, c * tc:(c + 1) * tc]
        m_new = jnp.maximum(m_prev, jnp.max(st, axis=0, keepdims=True))
        m_sc[h, :, c * tc:(c + 1) * tc] = m_new
        return jnp.exp2(st - m_new).astype(vt.dtype), jnp.exp2(m_prev - m_new)

    def accumulate(u, p, alpha):
        h, c = units[u]
        acc = acc_sc[h, :, c * tc:(c + 1) * tc]
        acc_sc[h, :, c * tc:(c + 1) * tc] = alpha * acc + jnp.dot(vt, p, preferred_element_type=F32)

    st, pa = {}, {}
    lag_b, lag_c = ATTN_SOFTMAX_LAG[kind], ATTN_SOFTMAX_LAG[kind] + ATTN_VALUE_LAG[kind]
    for i in range(len(units) + lag_c):
        if i < len(units):
            st[i] = scores_t(i)
        if 0 <= i - lag_b < len(units):
            pa[i - lag_b] = softmax_t(i - lag_b, st.pop(i - lag_b))
        if 0 <= i - lag_c < len(units):
            accumulate(i - lag_c, *pa.pop(i - lag_c))

    @pl.when(kv == n_kv - 1)
    def _():
        def normalised(h):
            acc = acc_sc[h]
            return acc[:dv] / acc[dv:dv + 1]

        if kind == "gqa":
            for h in range(n_q):
                o_ref[:, h * HEAD_DIM:(h + 1) * HEAD_DIM] = normalised(h).T.astype(o_ref.dtype)
        else:
            lam = (jnp.exp(jnp.sum(lq1[...] * lk1[...], axis=-1, keepdims=True))
                   - jnp.exp(jnp.sum(lq2[...] * lk2[...], axis=-1, keepdims=True)) + lam_init)
            o = (normalised(0) - lam * normalised(1)).T
            ms = jnp.mean(o * o, axis=-1, keepdims=True)
            o = o * lax.rsqrt(ms + NORM_EPS) * sg_ref[...] * (1.0 - lam_init)
            o_ref[...] = o.astype(o_ref.dtype)


def _attention(q, k, vt, kind, *, lam_params=None, sub_gain=None, lam_init=0.0):
    lq, d = q.shape
    t = k.shape[0]
    if kind == "gqa":
        n_heads = d // (GQA_GROUP * HEAD_DIM)
        qw, kw, vw, n_q = GQA_GROUP * HEAD_DIM, HEAD_DIM, HEAD_DIM, GQA_GROUP
    else:
        n_heads = d // (2 * HEAD_DIM)
        qw, kw, vw, n_q = 2 * HEAD_DIM, 2 * HEAD_DIM, 2 * HEAD_DIM, 2
    tq = _pick(lq, 1024, ATTN_COL_TILE)
    tk = _pick(t, ATTN_KEY_TILE[kind], LANES)
    n_kv = t // tk
    vr = vw + BF16_ROWS
    assert vt.shape == (n_heads, vr, t), (vt.shape, n_heads, vr, t)
    in_specs = [
        pl.BlockSpec((tq, qw), lambda h, i, j: (i, h)),
        pl.BlockSpec((tk, kw), lambda h, i, j: (j, h)),
        pl.BlockSpec((None, vr, tk), lambda h, i, j: (h, 0, j)),
    ]
    ops = [q, k, vt]
    if kind == "diff":
        vec = pl.BlockSpec((1, HEAD_DIM), lambda h, i, j: (0, 0))
        in_specs += [vec, vec, vec, vec, pl.BlockSpec((1, 2 * HEAD_DIM), lambda h, i, j: (0, 0))]
        ops += [p.reshape(1, HEAD_DIM) for p in lam_params] + [sub_gain.reshape(1, 2 * HEAD_DIM)]
    return pl.pallas_call(
        functools.partial(_attn_kernel, kind=kind, n_kv=n_kv, lam_init=lam_init),
        grid=(n_heads, lq // tq, n_kv),
        in_specs=in_specs,
        out_specs=pl.BlockSpec((tq, qw), lambda h, i, j: (i, h)),
        out_shape=jax.ShapeDtypeStruct((lq, d), BF16),
        scratch_shapes=[
            pltpu.VMEM((n_q, 1, tq), F32),
            pltpu.VMEM((n_q, vr, tq), F32),
        ],
        compiler_params=_params("parallel", "parallel", "arbitrary"),
        name="attn_" + kind,
    )(*ops)


def _sconv_kernel(z_ref, zp_ref, zn_ref, w_ref, b_ref, o_ref, *, n_row_blocks):
    i = pl.program_id(1)
    z = z_ref[...]
    tr = z.shape[0]
    prev_row = jnp.where(i == 0, 0.0, zp_ref[SUBLANES - 1:SUBLANES, :])
    next_row = jnp.where(i == n_row_blocks - 1, 0.0, zn_ref[0:1, :])
    row = lax.broadcasted_iota(jnp.int32, z.shape, 0)
    z_before = jnp.where(row == 0, prev_row, pltpu.roll(z, 1, axis=0))
    z_after = jnp.where(row == tr - 1, next_row, pltpu.roll(z, tr - 1, axis=0))
    w = w_ref[...]
    o_ref[0] = b_ref[...] + z_before * w[0:1] + z * w[1:2] + z_after * w[2:3]


def _short_conv(z, conv_w, conv_b, n_split):
    l, c = z.shape
    d = c // n_split
    tr = _pick(l, 512, SUBLANES)
    tc = _pick(d, 1024, LANES)
    ncb = d // tc
    nrb = l // tr
    r8 = tr // SUBLANES
    last8 = l // SUBLANES - 1
    return pl.pallas_call(
        functools.partial(_sconv_kernel, n_row_blocks=nrb),
        grid=(n_split, nrb, ncb),
        in_specs=[
            pl.BlockSpec((tr, tc), lambda s, i, j: (i, s * ncb + j)),
            pl.BlockSpec((SUBLANES, tc), lambda s, i, j: (jnp.maximum(i * r8 - 1, 0), s * ncb + j)),
            pl.BlockSpec((SUBLANES, tc), lambda s, i, j: (jnp.minimum((i + 1) * r8, last8), s * ncb + j)),
            pl.BlockSpec((conv_w.shape[0], tc), lambda s, i, j: (0, s * ncb + j)),
            pl.BlockSpec((1, tc), lambda s, i, j: (0, s * ncb + j)),
        ],
        out_specs=pl.BlockSpec((1, tr, tc), lambda s, i, j: (s, i, j)),
        out_shape=jax.ShapeDtypeStruct((n_split, l, d), F32),
        compiler_params=_params("parallel", "parallel", "parallel"),
        name="short_conv",
    )(z, z, z, conv_w, conv_b.reshape(1, c))


def _filter_mlp_kernel(f_ref, w1_ref, b1_ref, w2_ref, b2_ref, fr_ref, o_ref):
    g = jnp.dot(f_ref[...], w1_ref[...], precision=HIGHEST, preferred_element_type=F32)
    g = jnp.sin(fr_ref[0:1] * (g + b1_ref[...]))
    g = jnp.dot(g, w2_ref[...], precision=HIGHEST, preferred_element_type=F32)
    o_ref[...] = jnp.sin(fr_ref[1:2] * (g + b2_ref[...]))


def _filter_taps_fft1_kernel(g_ref, w3f_ref, w3b_ref, dl_ref, f_ref, tc_ref, ts_ref,
                             are_ref, aim_ref, asum_ref, *, n_tok, n2_size, k1p):
    n2 = pl.program_id(1)
    n1_size = g_ref.shape[0]
    half = n1_size // 2
    g = g_ref[...]

    def dot3(x, w_ref):
        hi = x.astype(BF16)
        lo = (x - hi.astype(F32)).astype(BF16)
        return jnp.dot(jnp.concatenate([hi, lo, hi], axis=1), w_ref[...], preferred_element_type=F32)

    taps = jnp.concatenate([dot3(g[:half], w3f_ref), dot3(g[half:], w3b_ref)], axis=0)
    n = lax.broadcasted_iota(jnp.int32, (n1_size, 1), 0) * n2_size + n2
    tau = jnp.where(n < n_tok, n, 2 * n_tok - n)
    t = tau.astype(F32) * (1.0 / (n_tok - 1))
    taps = taps * jnp.exp(-t * dl_ref[...])
    taps = jnp.where(n == n_tok, 0.0, taps)

    @pl.when(n2 == 0)
    def _():
        asum_ref[...] = jnp.zeros(asum_ref.shape, F32)

    asum_ref[...] += jnp.sum(jnp.abs(taps), axis=0, keepdims=True)

    a = jnp.dot(f_ref[...], _split3(taps), preferred_element_type=F32)
    re, im = a[:k1p], a[k1p:]
    c = _table_column(tc_ref, n2)
    s = _table_column(ts_ref, n2)
    are_ref[...] = re * c + im * s
    aim_ref[...] = im * c - re * s


def _hyena_filter_stage1(n_tok, d, consts, f_w1, f_b1, f_w2, f_b2, f_w3, f_freq):
    width = f_w1.shape[1]
    t = jnp.linspace(0.0, 1.0, n_tok, dtype=F32)[:, None]
    bands = (HY_EMB_DIM - 1) // 2
    w = (2.0 * math.pi / n_tok) * jnp.arange(n_tok, dtype=F32)[:, None]
    f = jnp.linspace(1e-4, bands - 1, bands, dtype=F32)[None, :]
    feats = jnp.concatenate([t, jnp.cos(f * w), -jnp.sin(f * w)], axis=-1)
    feats2 = jnp.concatenate([feats, feats[:1], feats[:0:-1]], axis=0)
    feats2 = jnp.pad(feats2, ((0, 0), (0, LANES - HY_EMB_DIM)))
    pw = LANES - width
    w1 = jnp.pad(f_w1, ((0, LANES - HY_EMB_DIM), (0, pw)))
    w2 = jnp.pad(f_w2, ((0, pw), (0, pw)))
    b1 = jnp.pad(f_b1, (0, pw)).reshape(1, LANES)
    b2 = jnp.pad(f_b2, (0, pw)).reshape(1, LANES)
    freq = jnp.pad(f_freq, ((0, 0), (0, pw)))
    n2l = 2 * n_tok
    tr = _pick(n_tok, 512, SUBLANES)
    full = lambda shape: pl.BlockSpec(shape, lambda i: (0,) * len(shape))
    g = pl.pallas_call(
        _filter_mlp_kernel,
        grid=(n2l // tr,),
        in_specs=[pl.BlockSpec((tr, LANES), lambda i: (i, 0)), full((LANES, LANES)), full((1, LANES)),
                  full((LANES, LANES)), full((1, LANES)), full((2, LANES))],
        out_specs=pl.BlockSpec((tr, LANES), lambda i: (i, 0)),
        out_shape=jax.ShapeDtypeStruct((n2l, LANES), F32),
        compiler_params=_params("parallel"),
        name="hyena_filter_mlp",
    )(feats2, w1, b1, w2, b2, freq)

    c = HY_ORDER * d
    w3 = jnp.pad(f_w3, ((0, pw), (0, 0)))
    w3_hi = w3.astype(BF16)
    w3 = jnp.concatenate([w3_hi, w3_hi, (w3 - w3_hi.astype(F32)).astype(BF16)], axis=0)
    max_decay = math.log(HY_DECAY_TARGET) / HY_FAST_DECAY_PCT
    min_decay = math.log(HY_DECAY_TARGET) / HY_SLOW_DECAY_PCT
    deltas = jnp.abs(jnp.linspace(min_decay, max_decay, d, dtype=F32))
    deltas = jnp.tile(deltas, HY_ORDER).reshape(1, c)
    _, n1, n2, _, k1p = _dft_plan(n_tok)
    g_by_n2 = g.reshape(n1, n2, LANES).transpose(1, 0, 2)
    tcol = _pick(c, 2048, LANES)
    ncb = c // tcol
    tab = pl.BlockSpec((k1p, n2), lambda j, i: (0, 0))
    out = pl.BlockSpec((k1p, tcol), lambda j, i: (0, i * ncb + j))
    return pl.pallas_call(
        functools.partial(_filter_taps_fft1_kernel, n_tok=n_tok, n2_size=n2, k1p=k1p),
        grid=(ncb, n2),
        in_specs=[
            pl.BlockSpec((None, n1, LANES), lambda j, i: (i, 0, 0)),
            pl.BlockSpec((3 * LANES, tcol), lambda j, i: (0, j)),
            pl.BlockSpec((3 * LANES, tcol), lambda j, i: (0, ncb + j)),
            pl.BlockSpec((1, tcol), lambda j, i: (0, j)),
            pl.BlockSpec((2 * k1p, 3 * n1), lambda j, i: (0, 0)), tab, tab,
        ],
        out_specs=[out, out, pl.BlockSpec((1, tcol), lambda j, i: (0, j))],
        out_shape=[jax.ShapeDtypeStruct((k1p, n2 * c), F32)] * 2 + [jax.ShapeDtypeStruct((1, c), F32)],
        compiler_params=_params("parallel", "arbitrary"),
        name="hyena_filter_fft1",
    )(g_by_n2, w3, w3, deltas, consts["f1_full"], consts["tw_cos"], consts["tw_sin"])


def _split3(x):
    hi = x.astype(BF16)
    lo = (x - hi.astype(F32)).astype(BF16)
    return jnp.concatenate([hi, lo, hi], axis=0)


def _mat3(m64):
    m = jnp.asarray(m64, F32)
    hi = m.astype(BF16)
    lo = (m - hi.astype(F32)).astype(BF16)
    return jnp.concatenate([hi, hi, lo], axis=1)


def _dft_plan(n_tok):
    n = 2 * n_tok
    n1 = 1 << ((n.bit_length() - 1 + 1) // 2)
    n2 = n // n1
    assert n1 * n2 == n and n2 >= SUBLANES, (n_tok, n1, n2)
    k1 = n1 // 2 + 1
    k1p = -(-k1 // BF16_ROWS) * BF16_ROWS
    return n, n1, n2, k1, k1p


def _dft_constants(n_tok):
    n, n1, n2, k1, k1p = _dft_plan(n_tok)
    two_pi = 2.0 * np.pi
    kk = np.arange(k1p)[:, None]
    valid = (kk < k1).astype(np.float64)
    ang = two_pi * ((kk * np.arange(n1)[None, :]) % n1) / n1
    f1 = np.concatenate([np.cos(ang) * valid, -np.sin(ang) * valid], axis=0)
    a2 = two_pi * ((np.arange(n2)[:, None] * np.arange(n2)[None, :]) % n2) / n2
    c2, s2 = np.cos(a2), np.sin(a2)
    f2 = np.block([[c2, s2], [-s2, c2]])
    g2 = np.block([[c2, -s2], [s2, c2]])
    wk = np.where((np.arange(k1p) == 0) | (np.arange(k1p) == n1 // 2), 1.0, 2.0) * (np.arange(k1p) < k1)
    a3 = two_pi * ((np.arange(n1 // 2)[:, None] * np.arange(k1p)[None, :]) % n1) / n1
    g3 = np.concatenate([np.cos(a3) * wk[None, :], -np.sin(a3) * wk[None, :]], axis=1) / n
    at = two_pi * (kk * np.arange(n2)[None, :]) / n
    return dict(
        f1_full=_mat3(f1), f1_half=_mat3(f1[:, :n1 // 2]), f2=_mat3(f2), g2=_mat3(g2), g3=_mat3(g3),
        tw_cos=jnp.asarray(np.cos(at), F32), tw_sin=jnp.asarray(np.sin(at), F32),
        tw_cos_t=jnp.asarray(np.cos(at).T, F32), tw_sin_t=jnp.asarray(np.sin(at).T, F32),
    )


def _table_column(tab_ref, idx):
    lane = lax.broadcasted_iota(jnp.int32, tab_ref.shape, 1)
    return jnp.sum(jnp.where(lane == idx, tab_ref[...], 0.0), axis=1, keepdims=True)


def _fft1_kernel(x_ref, f_ref, tc_ref, ts_ref, are_ref, aim_ref, *, k1p, blocks_per_n2):
    n2 = pl.program_id(0) // blocks_per_n2
    a = jnp.dot(f_ref[...], _split3(x_ref[...]), preferred_element_type=F32)
    re, im = a[:k1p], a[k1p:]
    c = _table_column(tc_ref, n2)
    s = _table_column(ts_ref, n2)
    are_ref[...] = re * c + im * s
    aim_ref[...] = im * c - re * s


def _fft_stage1(x2d, f1, consts, n_tok, n_ch):
    _, n1, n2, _, k1p = _dft_plan(n_tok)
    n1c, cols = x2d.shape
    tcol = _pick(n_ch, 2048, LANES)
    tab = pl.BlockSpec((k1p, n2), lambda j: (0, 0))
    out = pl.BlockSpec((k1p, tcol), lambda j: (0, j))
    return pl.pallas_call(
        functools.partial(_fft1_kernel, k1p=k1p, blocks_per_n2=n_ch // tcol),
        grid=(cols // tcol,),
        in_specs=[pl.BlockSpec((n1c, tcol), lambda j: (0, j)),
                  pl.BlockSpec((2 * k1p, 3 * n1c), lambda j: (0, 0)), tab, tab],
        out_specs=[out, out],
        out_shape=[jax.ShapeDtypeStruct((k1p, cols), F32)] * 2,
        compiler_params=_params("parallel"),
        name="hyena_fft1",
    )(x2d, f1, consts["tw_cos"], consts["tw_sin"])


def _fft2_spec_kernel(are_ref, aim_ref, f_ref, asum_ref, sre_ref, sim_ref, *, tk1, n2):
    inv = 1.0 / (asum_ref[...] + NORM_EPS)

    def forward(r):
        a = jnp.concatenate([are_ref[r], aim_ref[r]], axis=0)
        return jnp.dot(f_ref[...], _split3(a), preferred_element_type=F32)

    def store(r, x):
        sre_ref[r] = x[:n2] * inv
        sim_ref[r] = x[n2:] * inv

    xs = {}
    for r in range(tk1 + DFT_ROW_LAG):
        if r < tk1:
            xs[r] = forward(r)
        if r >= DFT_ROW_LAG:
            store(r - DFT_ROW_LAG, xs.pop(r - DFT_ROW_LAG))


def _fft2_conv_kernel(are_ref, aim_ref, sre_ref, sim_ref, f_ref, g_ref, tc_ref, ts_ref,
                      bre_ref, bim_ref, *, tk1, n2):
    i = pl.program_id(0)

    def forward(r):
        a = jnp.concatenate([are_ref[r], aim_ref[r]], axis=0)
        return jnp.dot(f_ref[...], _split3(a), preferred_element_type=F32)

    def inverse(r, x):
        xr, xi = x[:n2], x[n2:]
        sr, si = sre_ref[r], sim_ref[r]
        y = jnp.concatenate([xr * sr - xi * si, xr * si + xi * sr], axis=0)
        return jnp.dot(g_ref[...], _split3(y), preferred_element_type=F32)

    def store(r, b):
        br, bi = b[:n2], b[n2:]
        c = _table_column(tc_ref, i * tk1 + r)
        s = _table_column(ts_ref, i * tk1 + r)
        bre_ref[r] = br * c - bi * s
        bim_ref[r] = bi * c + br * s

    xs, bs = {}, {}
    for r in range(tk1 + 2 * DFT_ROW_LAG):
        if r < tk1:
            xs[r] = forward(r)
        if 0 <= r - DFT_ROW_LAG < tk1:
            bs[r - DFT_ROW_LAG] = inverse(r - DFT_ROW_LAG, xs.pop(r - DFT_ROW_LAG))
        if 0 <= r - 2 * DFT_ROW_LAG < tk1:
            store(r - 2 * DFT_ROW_LAG, bs.pop(r - 2 * DFT_ROW_LAG))


def _fft_stage2(are, aim, consts, n_tok, n_ch, *, spec=None, asum=None, spec_offset=0):
    _, n1, n2, _, k1p = _dft_plan(n_tok)
    tk1 = SUBLANES
    td = _pick(n_ch, 512, LANES)
    blk = pl.BlockSpec((tk1, n2, td), lambda i, j: (i, 0, j))
    mat = pl.BlockSpec((2 * n2, 6 * n2), lambda i, j: (0, 0))
    grid = (k1p // tk1, n_ch // td)
    out_shape = [jax.ShapeDtypeStruct((k1p, n2, n_ch), F32)] * 2
    if spec is None:
        return pl.pallas_call(
            functools.partial(_fft2_spec_kernel, tk1=tk1, n2=n2),
            grid=grid,
            in_specs=[blk, blk, mat, pl.BlockSpec((1, td), lambda i, j: (0, j))],
            out_specs=[blk, blk],
            out_shape=out_shape,
            compiler_params=_params("parallel", "parallel"),
            name="hyena_fft2_spec",
        )(are, aim, consts["f2"], asum)
    off = spec_offset // td
    sblk = pl.BlockSpec((tk1, n2, td), lambda i, j: (i, 0, j + off))
    tab = pl.BlockSpec((n2, k1p), lambda i, j: (0, 0))
    return pl.pallas_call(
        functools.partial(_fft2_conv_kernel, tk1=tk1, n2=n2),
        grid=grid,
        in_specs=[blk, blk, sblk, sblk, mat, mat, tab, tab],
        out_specs=[blk, blk],
        out_shape=out_shape,
        compiler_params=_params("parallel", "parallel"),
        name="hyena_fft2_conv",
    )(are, aim, spec[0], spec[1], consts["f2"], consts["g2"], consts["tw_cos_t"], consts["tw_sin_t"])


def _fft3_kernel(bre_ref, bim_ref, g_ref, y_ref, gate_ref, bias_ref, o_ref):
    b = jnp.concatenate([bre_ref[...], bim_ref[...]], axis=0)
    conv = jnp.dot(g_ref[...], _split3(b), preferred_element_type=F32)
    o_ref[...] = (gate_ref[...] * (conv + y_ref[...] * bias_ref[...])).astype(o_ref.dtype)


def _fft_stage3(bre, bim, consts, n_tok, n_ch, y2d, gate2d, bias, out_dtype):
    _, n1, n2, _, k1p = _dft_plan(n_tok)
    cols = n2 * n_ch
    tcol = _pick(n_ch, 2048, LANES)
    per = n_ch // tcol
    inb = pl.BlockSpec((k1p, tcol), lambda j: (0, j))
    sig = pl.BlockSpec((n1 // 2, tcol), lambda j: (0, j))
    return pl.pallas_call(
        _fft3_kernel,
        grid=(cols // tcol,),
        in_specs=[inb, inb, pl.BlockSpec((n1 // 2, 6 * k1p), lambda j: (0, 0)), sig, sig,
                  pl.BlockSpec((1, tcol), lambda j: (0, j % per))],
        out_specs=sig,
        out_shape=jax.ShapeDtypeStruct((n1 // 2, cols), out_dtype),
        compiler_params=_params("parallel"),
        name="hyena_fft3",
    )(bre, bim, consts["g3"], y2d, gate2d, bias.reshape(1, n_ch))


def _hyena_mixer_core(h, w_in, b_in, conv_w, conv_b, f_w1, f_b1, f_w2, f_b2, f_w3, f_freq, f_bias):
    n_tok, d = h.shape
    _, n1, n2, _, k1p = _dft_plan(n_tok)
    consts = _dft_constants(n_tok)
    z = _matmul(h, (w_in,), "bias", F32, extra=(b_in,))
    zs = _short_conv(z, conv_w, conv_b, HY_ORDER + 1)

    c = HY_ORDER * d
    tre, tim, asum = _hyena_filter_stage1(n_tok, d, consts, f_w1, f_b1, f_w2, f_b2, f_w3, f_freq)
    spec = _fft_stage2(tre.reshape(k1p, n2, c), tim.reshape(k1p, n2, c), consts, n_tok, c, asum=asum)

    y = zs[0].reshape(n1 // 2, n2 * d)
    for n in range(HY_ORDER):
        are, aim = _fft_stage1(y, consts["f1_half"], consts, n_tok, d)
        bre, bim = _fft_stage2(are.reshape(k1p, n2, d), aim.reshape(k1p, n2, d), consts, n_tok, d,
                               spec=spec, spec_offset=n * d)
        last = n == HY_ORDER - 1
        y = _fft_stage3(bre.reshape(k1p, n2 * d), bim.reshape(k1p, n2 * d), consts, n_tok, d, y,
                        zs[n + 1].reshape(n1 // 2, n2 * d), f_bias[n], BF16 if last else F32)
    return y.reshape(n_tok, d)


def _axial_rope_tables(n_tok):
    rows = n_tok // GRID_W
    row = jnp.repeat(jnp.arange(rows, dtype=F32), GRID_W)
    col = (jnp.arange(rows * GRID_W) % GRID_W).astype(F32)
    n_pair = HEAD_DIM // 4
    inv_freq = ROPE_THETA ** (-jnp.arange(n_pair, dtype=F32) / n_pair)
    ang = jnp.concatenate([row[:, None] * inv_freq, col[:, None] * inv_freq], axis=-1)
    cos, sin = jnp.cos(ang), jnp.sin(ang)
    signed_sin = jnp.stack([-sin, sin], axis=-1).reshape(sin.shape[0], HEAD_DIM)
    return jnp.repeat(cos, 2, axis=-1), signed_sin


def kernel(x, c, ctx, c_ctx, ada_a, ada_b, ada_bias, norm1_g, norm2_g, ffn_w_gate, ffn_w_up, ffn_w_down,
           hy_w_in, hy_b_in, hy_conv_w, hy_conv_b, hy_f_w1, hy_f_b1, hy_f_w2, hy_f_b2, hy_f_w3, hy_f_freq,
           hy_f_bias, hy_w_out, gqa_w_q, gqa_w_k, gqa_w_v, gqa_q_norm, gqa_k_norm, gqa_w_o,
           diff_w_q, diff_w_k, diff_w_v, diff_q_norm, diff_k_norm, diff_lambda_q1, diff_lambda_k1,
           diff_lambda_q2, diff_lambda_k2, diff_sub_norm, diff_w_o):
    assert x.shape[0] == 1 and ctx.shape[0] == 1
    depth = ada_a.shape[0]
    n_tok, d = x.shape[1], x.shape[2]
    xs = x[0]
    cs = ctx[0]
    cos, sin = _axial_rope_tables(n_tok)
    n_ctx = ctx.shape[1]
    cos_all = jnp.concatenate([cos, jnp.ones((n_ctx, HEAD_DIM), F32)], axis=0)
    sin_all = jnp.concatenate([sin, jnp.zeros((n_ctx, HEAD_DIM), F32)], axis=0)

    def keys_values(h, hc, w_k, w_v, k_gain, head_width):
        h_all = jnp.concatenate([h, hc], axis=0)
        t = h_all.shape[0]
        tm = _pick(t, 1280, MM_ROW_CHUNK) if t % MM_ROW_CHUNK == 0 else t
        k = _matmul(h_all, (w_k,), "headrope", BF16, extra=(k_gain, cos_all, sin_all), tm=tm)
        vt = _matmul(h_all, (w_v,), "values_t", BF16, tm=tm, head_width=head_width)
        return k, vt

    cc = jnp.concatenate([c.reshape(1, d), c_ctx.reshape(1, d), jnp.zeros((SUBLANES - 2, d), F32)], axis=0)
    mods = _ada_mods(cc, ada_a, ada_b, ada_bias)

    kinds = [i % N_MIXERS for i in range(depth)]
    for i in range(depth):
        kind = kinds[i]
        j = i // N_MIXERS
        reads_ctx = kind != MIX_HYENA
        update_ctx = any(kk != MIX_HYENA for kk in kinds[i + 1:])
        sh1, sc1, g1, sh2, sc2, g2 = [mods[i, 0, m * d:(m + 1) * d] for m in range(N_MOD)]
        csh1, csc1, cg1, csh2, csc2, cg2 = [mods[i, 1, m * d:(m + 1) * d] for m in range(N_MOD)]
        h = _norm_mod(xs, norm1_g[i], sh1, sc1)
        hc = _norm_mod(cs, norm1_g[i], csh1, csc1) if (reads_ctx or update_ctx) else None
        yc = None
        if kind == MIX_HYENA:
            w_in = hy_w_in[j].astype(BF16)
            w_out = hy_w_out[j].astype(BF16)
            args = (w_in, hy_b_in[j], hy_conv_w[j], hy_conv_b[j], hy_f_w1[j], hy_f_b1[j], hy_f_w2[j],
                    hy_f_b2[j], hy_f_w3[j], hy_f_freq[j], hy_f_bias[j])
            y = _hyena_mixer_core(h, *args)
            xs = _matmul(y, (w_out,), "resid", F32, extra=(xs, g1))
            if update_ctx:
                yc = _hyena_mixer_core(hc, *args)
                cs = _matmul(yc, (w_out,), "resid", F32, extra=(cs, cg1))
        elif kind == MIX_GQA:
            w_q = gqa_w_q[j].astype(BF16)
            w_k = gqa_w_k[j].astype(BF16)
            w_v = gqa_w_v[j].astype(BF16)
            w_o = gqa_w_o[j].astype(BF16)
            qg, kg = gqa_q_norm[j], gqa_k_norm[j]
            q = _matmul(h, (w_q,), "headrope", BF16, extra=(qg, cos, sin), out_scale=ATTN_Q_SCALE)
            k, vt = keys_values(h, hc, w_k, w_v, kg, HEAD_DIM)
            o = _attention(q, k, vt, "gqa")
            xs = _matmul(o, (w_o,), "resid", F32, extra=(xs, g1))
            if update_ctx:
                qc = _matmul(hc, (w_q,), "headnorm", BF16, extra=(qg,), out_scale=ATTN_Q_SCALE)
                oc = _attention(qc, k[n_tok:], vt[:, :, n_tok:], "gqa")
                cs = _matmul(oc, (w_o,), "resid", F32, extra=(cs, cg1))
        else:
            w_q = diff_w_q[j].astype(BF16)
            w_k = diff_w_k[j].astype(BF16)
            w_v = diff_w_v[j].astype(BF16)
            w_o = diff_w_o[j].astype(BF16)
            qg, kg = diff_q_norm[j], diff_k_norm[j]
            lam_init = 0.8 - 0.6 * math.exp(-0.3 * i)
            lam_params = (diff_lambda_q1[j], diff_lambda_k1[j], diff_lambda_q2[j], diff_lambda_k2[j])
            attend = functools.partial(_attention, kind="diff", lam_params=lam_params,
                                       sub_gain=diff_sub_norm[j], lam_init=lam_init)
            q = _matmul(h, (w_q,), "headrope", BF16, extra=(qg, cos, sin), out_scale=ATTN_Q_SCALE)
            k, vt = keys_values(h, hc, w_k, w_v, kg, 2 * HEAD_DIM)
            o = attend(q, k, vt)
            xs = _matmul(o, (w_o,), "resid", F32, extra=(xs, g1))
            if update_ctx:
                qc = _matmul(hc, (w_q,), "headnorm", BF16, extra=(qg,), out_scale=ATTN_Q_SCALE)
                oc = attend(qc, k[n_tok:], vt[:, :, n_tok:])
                cs = _matmul(oc, (w_o,), "resid", F32, extra=(cs, cg1))

        w_gate = ffn_w_gate[i].astype(BF16)
        w_up = ffn_w_up[i].astype(BF16)
        w_down = ffn_w_down[i].astype(BF16)
        h2 = _norm_mod(xs, norm2_g[i], sh2, sc2)
        act = _matmul(h2, (w_gate, w_up), "swiglu", BF16, tn=256)
        xs = _matmul(act, (w_down,), "resid", F32, extra=(xs, g2), tm=512, tn=256)
        if update_ctx:
            hc2 = _norm_mod(cs, norm2_g[i], csh2, csc2)
            actc = _matmul(hc2, (w_gate, w_up), "swiglu", BF16, tn=256)
            cs = _matmul(actc, (w_down,), "resid", F32, extra=(cs, cg2), tm=512, tn=256)
    return xs[None]
```

```python
import functools
import math

import numpy as np
import jax
import jax.numpy as jnp
from jax import lax
from jax.experimental import pallas as pl
from jax.experimental.pallas import tpu as pltpu

HEAD_DIM = 128
GRID_W = 64
GQA_GROUP = 4
N_MOD = 6
N_MIXERS = 3
MIX_HYENA, MIX_GQA, MIX_DIFF = 0, 1, 2
NORM_EPS = 1e-6
ROPE_THETA = 10000.0
HY_ORDER = 2
HY_EMB_DIM = 33
HY_DECAY_TARGET = 1e-2
HY_FAST_DECAY_PCT = 0.3
HY_SLOW_DECAY_PCT = 1.5
ATTN_Q_SCALE = HEAD_DIM ** -0.5 * math.log2(math.e)
V7X_MXU_WIDTH = 256
ATTN_COL_TILE = V7X_MXU_WIDTH
ATTN_KEY_TILE = {"gqa": 1280, "diff": 3328}
ATTN_SOFTMAX_LAG = {"gqa": 3, "diff": 2}
ATTN_VALUE_LAG = {"gqa": 2, "diff": 1}
V7X_VMEM_BYTES = 64 * 1024 * 1024
VMEM_LIMIT_BYTES = V7X_VMEM_BYTES - 8 * 1024 * 1024
LANES = 128
SUBLANES = 8
BF16_ROWS = 16
DFT_ROW_LAG = 1
MM_ROW_CHUNK = 256

F32 = jnp.float32
BF16 = jnp.bfloat16
HIGHEST = lax.Precision.HIGHEST


def _pick(dim, pref, align):
    if dim <= pref:
        return dim
    best = None
    for d in range(align, pref + 1, align):
        if dim % d == 0:
            best = d
    assert best is not None, (dim, pref, align)
    return best


def _params(*sem):
    return pltpu.CompilerParams(dimension_semantics=sem, vmem_limit_bytes=VMEM_LIMIT_BYTES)


def _silu(x):
    return x * (1.0 / (1.0 + jnp.exp(-x)))


def _ada_kernel(cc_ref, a_ref, b_ref, bias_ref, o_ref):
    s = _silu(cc_ref[...])
    t = jnp.dot(s, a_ref[0], precision=HIGHEST, preferred_element_type=F32)
    o_ref[0] = jnp.dot(t, b_ref[0], precision=HIGHEST, preferred_element_type=F32) + bias_ref[0]


def _ada_mods(cc, ada_a, ada_b, ada_bias):
    depth, d, r = ada_a.shape
    n = ada_b.shape[2]
    tn = _pick(n, 2048, LANES)
    return pl.pallas_call(
        _ada_kernel,
        grid=(depth, n // tn),
        in_specs=[
            pl.BlockSpec((SUBLANES, d), lambda l, j: (0, 0)),
            pl.BlockSpec((1, d, r), lambda l, j: (l, 0, 0)),
            pl.BlockSpec((1, r, tn), lambda l, j: (l, 0, j)),
            pl.BlockSpec((1, 1, tn), lambda l, j: (l, 0, j)),
        ],
        out_specs=pl.BlockSpec((1, SUBLANES, tn), lambda l, j: (l, 0, j)),
        out_shape=jax.ShapeDtypeStruct((depth, SUBLANES, n), F32),
        compiler_params=_params("parallel", "parallel"),
        name="ada_mods",
    )(cc, ada_a, ada_b, ada_bias.reshape(depth, 1, n))


def _norm_mod_kernel(x_ref, g_ref, sh_ref, sc_ref, o_ref):
    x = x_ref[...]
    ms = jnp.mean(x * x, axis=-1, keepdims=True)
    y = x * lax.rsqrt(ms + NORM_EPS) * g_ref[...]
    o_ref[...] = (y * (1.0 + sc_ref[...]) + sh_ref[...]).astype(o_ref.dtype)


def _norm_mod(x, gain, shift, scale):
    m, d = x.shape
    tr = _pick(m, 256, SUBLANES)
    vec = pl.BlockSpec((1, d), lambda i: (0, 0))
    return pl.pallas_call(
        _norm_mod_kernel,
        grid=(m // tr,),
        in_specs=[pl.BlockSpec((tr, d), lambda i: (i, 0)), vec, vec, vec],
        out_specs=pl.BlockSpec((tr, d), lambda i: (i, 0)),
        out_shape=jax.ShapeDtypeStruct((m, d), BF16),
        compiler_params=_params("parallel"),
        name="norm_mod",
    )(x, gain.reshape(1, d), shift.reshape(1, d), scale.reshape(1, d))


def _head_norm_rope(y, gain, cos, sin, out_scale):
    outs = []
    if cos is not None:
        even = lax.broadcasted_iota(jnp.int32, cos.shape, 1) % 2 == 0
    for h in range(y.shape[1] // HEAD_DIM):
        t = y[:, h * HEAD_DIM:(h + 1) * HEAD_DIM]
        ms = jnp.mean(t * t, axis=-1, keepdims=True)
        t = t * lax.rsqrt(ms + NORM_EPS) * gain
        if cos is not None:
            partner = jnp.where(even, pltpu.roll(t, HEAD_DIM - 1, axis=1), pltpu.roll(t, 1, axis=1))
            t = t * cos + partner * sin
        if out_scale != 1.0:
            t = t * out_scale
        outs.append(t)
    return outs[0] if len(outs) == 1 else jnp.concatenate(outs, axis=1)


def _mm_kernel(*refs, mode, nk, out_scale, head_width):
    n_w = 2 if mode == "swiglu" else 1
    a_ref, w_refs = refs[0], refs[1:1 + n_w]
    n_extra = {"bias": 1, "plain": 0, "values_t": 0, "swiglu": 0, "resid": 2, "headnorm": 1, "headrope": 3}[mode]
    extra = refs[1 + n_w:1 + n_w + n_extra]
    o_ref = refs[1 + n_w + n_extra]
    accs = refs[2 + n_w + n_extra:]
    k = pl.program_id(2)

    def epilogue(ys, rows=slice(None)):
        y = ys[0]
        if mode == "bias":
            y = y + extra[0][...]
        elif mode == "swiglu":
            y = _silu(y) * ys[1]
        elif mode == "resid":
            y = extra[0][rows, :] + extra[1][...] * y
        elif mode == "headnorm":
            y = _head_norm_rope(y, extra[0][...], None, None, out_scale)
        elif mode == "headrope":
            y = _head_norm_rope(y, extra[0][...], extra[1][rows, :], extra[2][rows, :], out_scale)
        elif mode == "values_t":
            n_rows = y.shape[0]
            pad = lax.broadcasted_iota(jnp.int32, (BF16_ROWS, n_rows), 0) == 0
            for h in range(y.shape[1] // head_width):
                o_ref[h, :head_width, rows] = y[:, h * head_width:(h + 1) * head_width].T.astype(o_ref.dtype)
                o_ref[h, head_width:, rows] = pad.astype(o_ref.dtype)
            return
        o_ref[rows, :] = y.astype(o_ref.dtype)

    if nk == 1:
        tm = a_ref.shape[0]
        rc = MM_ROW_CHUNK if tm % MM_ROW_CHUNK == 0 else tm
        pending = None
        for r in range(0, tm, rc):
            rows = slice(r, r + rc)
            a = a_ref[rows, :]
            parts = [jnp.dot(a, w[...], preferred_element_type=F32) for w in w_refs]
            if pending is not None:
                epilogue(*pending)
            pending = (parts, rows)
        epilogue(*pending)
        return

    a = a_ref[...]
    parts = [jnp.dot(a, w[...], preferred_element_type=F32) for w in w_refs]

    @pl.when(k == 0)
    def _():
        for acc, p in zip(accs, parts):
            acc[...] = p

    @pl.when(jnp.logical_and(k > 0, k < nk - 1))
    def _():
        for acc, p in zip(accs, parts):
            acc[...] += p

    @pl.when(k == nk - 1)
    def _():
        epilogue([acc[...] + p for acc, p in zip(accs, parts)])


def _matmul(a, ws, mode, out_dtype, *, extra=(), tm=1024, tn=512, tk=None, out_scale=1.0, head_width=None):
    m, kdim = a.shape
    n = ws[0].shape[1]
    tm = _pick(m, tm, BF16_ROWS)
    tn = _pick(n, tn, LANES)
    tk = kdim if tk is None else _pick(kdim, tk, LANES)
    nk = kdim // tk
    grid = (m // tm, n // tn, nk)
    in_specs = [pl.BlockSpec((tm, tk), lambda i, j, k: (i, k))]
    in_specs += [pl.BlockSpec((tk, tn), lambda i, j, k: (k, j)) for _ in ws]
    row = pl.BlockSpec((1, tn), lambda i, j, k: (0, j))
    head = pl.BlockSpec((1, HEAD_DIM), lambda i, j, k: (0, 0))
    ops = []
    if mode == "bias":
        in_specs += [row]
        ops = [extra[0].reshape(1, n)]
    elif mode == "resid":
        in_specs += [pl.BlockSpec((tm, tn), lambda i, j, k: (i, j)), row]
        ops = [extra[0], extra[1].reshape(1, n)]
    elif mode == "headnorm":
        in_specs += [head]
        ops = [extra[0].reshape(1, HEAD_DIM)]
    elif mode == "headrope":
        tab = pl.BlockSpec((tm, HEAD_DIM), lambda i, j, k: (i, 0))
        in_specs += [head, tab, tab]
        ops = [extra[0].reshape(1, HEAD_DIM), extra[1], extra[2]]
    n_acc = len(ws) if nk > 1 else 0
    out_spec = pl.BlockSpec((tm, tn), lambda i, j, k: (i, j))
    out_shape = jax.ShapeDtypeStruct((m, n), out_dtype)
    if mode == "values_t":
        assert nk == 1 and tn % head_width == 0 and tm % LANES == 0
        out_spec = pl.BlockSpec((tn // head_width, head_width + BF16_ROWS, tm), lambda i, j, k: (j, 0, i))
        out_shape = jax.ShapeDtypeStruct((n // head_width, head_width + BF16_ROWS, m), out_dtype)
    return pl.pallas_call(
        functools.partial(_mm_kernel, mode=mode, nk=nk, out_scale=out_scale, head_width=head_width),
        grid=grid,
        in_specs=in_specs,
        out_specs=out_spec,
        out_shape=out_shape,
        scratch_shapes=[pltpu.VMEM((tm, tn), F32) for _ in range(n_acc)],
        compiler_params=pltpu.CompilerParams(
            dimension_semantics=("parallel", "parallel", "arbitrary"), vmem_limit_bytes=VMEM_LIMIT_BYTES,
            allow_input_fusion=[False] + [True] * len(ws) + [False] * len(ops)),
        name="mm_" + mode,
    )(a, *ws, *ops)


def _attn_kernel(*refs, kind, n_kv, lam_init):
    if kind == "gqa":
        q_ref, k_ref, vt_ref, o_ref, m_sc, acc_sc = refs
        n_q = GQA_GROUP
    else:
        q_ref, k_ref, vt_ref, lq1, lk1, lq2, lk2, sg_ref, o_ref, m_sc, acc_sc = refs
        n_q = 2
    kv = pl.program_id(2)
    dv = vt_ref.shape[0] - BF16_ROWS

    @pl.when(kv == 0)
    def _():
        m_sc[...] = jnp.full(m_sc.shape, -jnp.inf, F32)
        acc_sc[...] = jnp.zeros(acc_sc.shape, F32)

    vt = vt_ref[...]
    tq = q_ref.shape[0]
    tc = min(tq, ATTN_COL_TILE)
    units = [(h, c) for h in range(n_q) for c in range(tq // tc)]

    def scores_t(u):
        h, c = units[u]
        q = q_ref[c * tc:(c + 1) * tc, h * HEAD_DIM:(h + 1) * HEAD_DIM]
        kh = 0 if kind == "gqa" else h
        k = k_ref[:, kh * HEAD_DIM:(kh + 1) * HEAD_DIM]
        return lax.dot_general(k, q, (((1,), (1,)), ((), ())), preferred_element_type=F32)

    def softmax_t(u, st):
        h, c = units[u]
        m_prev = m_sc[h, :, c * tc:(c + 1) * tc]
        m_new = jnp.maximum(m_prev, jnp.max(st, axis=0, keepdims=True))
        m_sc[h, :, c * tc:(c + 1) * tc] = m_new
        return jnp.exp2(st - m_new).astype(vt.dtype), jnp.exp2(m_prev - m_new)

    def accumulate(u, p, alpha):
        h, c = units[u]
        acc = acc_sc[h, :, c * tc:(c + 1) * tc]
        acc_sc[h, :, c * tc:(c + 1) * tc] = alpha * acc + jnp.dot(vt, p, preferred_element_type=F32)

    st, pa = {}, {}
    lag_b, lag_c = ATTN_SOFTMAX_LAG[kind], ATTN_SOFTMAX_LAG[kind] + ATTN_VALUE_LAG[kind]
    for i in range(len(units) + lag_c):
        if i < len(units):
            st[i] = scores_t(i)
        if 0 <= i - lag_b < len(units):
            pa[i - lag_b] = softmax_t(i - lag_b, st.pop(i - lag_b))
        if 0 <= i - lag_c < len(units):
            accumulate(i - lag_c, *pa.pop(i - lag_c))

    @pl.when(kv == n_kv - 1)
    def _():
        def normalised(h):
            acc = acc_sc[h]
            return acc[:dv] / acc[dv:dv + 1]

        if kind == "gqa":
            for h in range(n_q):
                o_ref[:, h * HEAD_DIM:(h + 1) * HEAD_DIM] = normalised(h).T.astype(o_ref.dtype)
        else:
            lam = (jnp.exp(jnp.sum(lq1[...] * lk1[...], axis=-1, keepdims=True))
                   - jnp.exp(jnp.sum(lq2[...] * lk2[...], axis=-1, keepdims=True)) + lam_init)
            o = (normalised(0) - lam * normalised(1)).T
            ms = jnp.mean(o * o, axis=-1, keepdims=True)
            o = o * lax.rsqrt(ms + NORM_EPS) * sg_ref[...] * (1.0 - lam_init)
            o_ref[...] = o.astype(o_ref.dtype)


def _attention(q, k, vt, kind, *, lam_params=None, sub_gain=None, lam_init=0.0):
    lq, d = q.shape
    t = k.shape[0]
    if kind == "gqa":
        n_heads = d // (GQA_GROUP * HEAD_DIM)
        qw, kw, vw, n_q = GQA_GROUP * HEAD_DIM, HEAD_DIM, HEAD_DIM, GQA_GROUP
    else:
        n_heads = d // (2 * HEAD_DIM)
        qw, kw, vw, n_q = 2 * HEAD_DIM, 2 * HEAD_DIM, 2 * HEAD_DIM, 2
    tq = _pick(lq, 1024, ATTN_COL_TILE)
    tk = _pick(t, ATTN_KEY_TILE[kind], LANES)
    n_kv = t // tk
    vr = vw + BF16_ROWS
    assert vt.shape == (n_heads, vr, t), (vt.shape, n_heads, vr, t)
    in_specs = [
        pl.BlockSpec((tq, qw), lambda h, i, j: (i, h)),
        pl.BlockSpec((tk, kw), lambda h, i, j: (j, h)),
        pl.BlockSpec((None, vr, tk), lambda h, i, j: (h, 0, j)),
    ]
    ops = [q, k, vt]
    if kind == "diff":
        vec = pl.BlockSpec((1, HEAD_DIM), lambda h, i, j: (0, 0))
        in_specs += [vec, vec, vec, vec, pl.BlockSpec((1, 2 * HEAD_DIM), lambda h, i, j: (0, 0))]
        ops += [p.reshape(1, HEAD_DIM) for p in lam_params] + [sub_gain.reshape(1, 2 * HEAD_DIM)]
    return pl.pallas_call(
        functools.partial(_attn_kernel, kind=kind, n_kv=n_kv, lam_init=lam_init),
        grid=(n_heads, lq // tq, n_kv),
        in_specs=in_specs,
        out_specs=pl.BlockSpec((tq, qw), lambda h, i, j: (i, h)),
        out_shape=jax.ShapeDtypeStruct((lq, d), BF16),
        scratch_shapes=[
            pltpu.VMEM((n_q, 1, tq), F32),
            pltpu.VMEM((n_q, vr, tq), F32),
        ],
        compiler_params=_params("parallel", "parallel", "arbitrary"),
        name="attn_" + kind,
    )(*ops)


def _sconv_kernel(z_ref, zp_ref, zn_ref, w_ref, b_ref, o_ref, *, n_row_blocks):
    i = pl.program_id(1)
    z = z_ref[...]
    tr = z.shape[0]
    prev_row = jnp.where(i == 0, 0.0, zp_ref[SUBLANES - 1:SUBLANES, :])
    next_row = jnp.where(i == n_row_blocks - 1, 0.0, zn_ref[0:1, :])
    row = lax.broadcasted_iota(jnp.int32, z.shape, 0)
    z_before = jnp.where(row == 0, prev_row, pltpu.roll(z, 1, axis=0))
    z_after = jnp.where(row == tr - 1, next_row, pltpu.roll(z, tr - 1, axis=0))
    w = w_ref[...]
    o_ref[0] = b_ref[...] + z_before * w[0:1] + z * w[1:2] + z_after * w[2:3]


def _short_conv(z, conv_w, conv_b, n_split):
    l, c = z.shape
    d = c // n_split
    tr = _pick(l, 512, SUBLANES)
    tc = _pick(d, 1024, LANES)
    ncb = d // tc
    nrb = l // tr
    r8 = tr // SUBLANES
    last8 = l // SUBLANES - 1
    return pl.pallas_call(
        functools.partial(_sconv_kernel, n_row_blocks=nrb),
        grid=(n_split, nrb, ncb),
        in_specs=[
            pl.BlockSpec((tr, tc), lambda s, i, j: (i, s * ncb + j)),
            pl.BlockSpec((SUBLANES, tc), lambda s, i, j: (jnp.maximum(i * r8 - 1, 0), s * ncb + j)),
            pl.BlockSpec((SUBLANES, tc), lambda s, i, j: (jnp.minimum((i + 1) * r8, last8), s * ncb + j)),
            pl.BlockSpec((conv_w.shape[0], tc), lambda s, i, j: (0, s * ncb + j)),
            pl.BlockSpec((1, tc), lambda s, i, j: (0, s * ncb + j)),
        ],
        out_specs=pl.BlockSpec((1, tr, tc), lambda s, i, j: (s, i, j)),
        out_shape=jax.ShapeDtypeStruct((n_split, l, d), F32),
        compiler_params=_params("parallel", "parallel", "parallel"),
        name="short_conv",
    )(z, z, z, conv_w, conv_b.reshape(1, c))


def _filter_mlp_kernel(f_ref, w1_ref, b1_ref, w2_ref, b2_ref, fr_ref, o_ref):
    g = jnp.dot(f_ref[...], w1_ref[...], precision=HIGHEST, preferred_element_type=F32)
    g = jnp.sin(fr_ref[0:1] * (g + b1_ref[...]))
    g = jnp.dot(g, w2_ref[...], precision=HIGHEST, preferred_element_type=F32)
    o_ref[...] = jnp.sin(fr_ref[1:2] * (g + b2_ref[...]))


def _filter_taps_fft1_kernel(g_ref, w3f_ref, w3b_ref, dl_ref, f_ref, tc_ref, ts_ref,
                             are_ref, aim_ref, asum_ref, *, n_tok, n2_size, k1p):
    n2 = pl.program_id(1)
    n1_size = g_ref.shape[0]
    half = n1_size // 2
    g = g_ref[...]

    def dot3(x, w_ref):
        hi = x.astype(BF16)
        lo = (x - hi.astype(F32)).astype(BF16)
        return jnp.dot(jnp.concatenate([hi, lo, hi], axis=1), w_ref[...], preferred_element_type=F32)

    taps = jnp.concatenate([dot3(g[:half], w3f_ref), dot3(g[half:], w3b_ref)], axis=0)
    n = lax.broadcasted_iota(jnp.int32, (n1_size, 1), 0) * n2_size + n2
    tau = jnp.where(n < n_tok, n, 2 * n_tok - n)
    t = tau.astype(F32) * (1.0 / (n_tok - 1))
    taps = taps * jnp.exp(-t * dl_ref[...])
    taps = jnp.where(n == n_tok, 0.0, taps)

    @pl.when(n2 == 0)
    def _():
        asum_ref[...] = jnp.zeros(asum_ref.shape, F32)

    asum_ref[...] += jnp.sum(jnp.abs(taps), axis=0, keepdims=True)

    a = jnp.dot(f_ref[...], _split3(taps), preferred_element_type=F32)
    re, im = a[:k1p], a[k1p:]
    c = _table_column(tc_ref, n2)
    s = _table_column(ts_ref, n2)
    are_ref[...] = re * c + im * s
    aim_ref[...] = im * c - re * s


def _hyena_filter_stage1(n_tok, d, consts, f_w1, f_b1, f_w2, f_b2, f_w3, f_freq):
    width = f_w1.shape[1]
    t = jnp.linspace(0.0, 1.0, n_tok, dtype=F32)[:, None]
    bands = (HY_EMB_DIM - 1) // 2
    w = (2.0 * math.pi / n_tok) * jnp.arange(n_tok, dtype=F32)[:, None]
    f = jnp.linspace(1e-4, bands - 1, bands, dtype=F32)[None, :]
    feats = jnp.concatenate([t, jnp.cos(f * w), -jnp.sin(f * w)], axis=-1)
    feats2 = jnp.concatenate([feats, feats[:1], feats[:0:-1]], axis=0)
    feats2 = jnp.pad(feats2, ((0, 0), (0, LANES - HY_EMB_DIM)))
    pw = LANES - width
    w1 = jnp.pad(f_w1, ((0, LANES - HY_EMB_DIM), (0, pw)))
    w2 = jnp.pad(f_w2, ((0, pw), (0, pw)))
    b1 = jnp.pad(f_b1, (0, pw)).reshape(1, LANES)
    b2 = jnp.pad(f_b2, (0, pw)).reshape(1, LANES)
    freq = jnp.pad(f_freq, ((0, 0), (0, pw)))
    n2l = 2 * n_tok
    tr = _pick(n_tok, 512, SUBLANES)
    full = lambda shape: pl.BlockSpec(shape, lambda i: (0,) * len(shape))
    g = pl.pallas_call(
        _filter_mlp_kernel,
        grid=(n2l // tr,),
        in_specs=[pl.BlockSpec((tr, LANES), lambda i: (i, 0)), full((LANES, LANES)), full((1, LANES)),
                  full((LANES, LANES)), full((1, LANES)), full((2, LANES))],
        out_specs=pl.BlockSpec((tr, LANES), lambda i: (i, 0)),
        out_shape=jax.ShapeDtypeStruct((n2l, LANES), F32),
        compiler_params=_params("parallel"),
        name="hyena_filter_mlp",
    )(feats2, w1, b1, w2, b2, freq)

    c = HY_ORDER * d
    w3 = jnp.pad(f_w3, ((0, pw), (0, 0)))
    w3_hi = w3.astype(BF16)
    w3 = jnp.concatenate([w3_hi, w3_hi, (w3 - w3_hi.astype(F32)).astype(BF16)], axis=0)
    max_decay = math.log(HY_DECAY_TARGET) / HY_FAST_DECAY_PCT
    min_decay = math.log(HY_DECAY_TARGET) / HY_SLOW_DECAY_PCT
    deltas = jnp.abs(jnp.linspace(min_decay, max_decay, d, dtype=F32))
    deltas = jnp.tile(deltas, HY_ORDER).reshape(1, c)
    _, n1, n2, _, k1p = _dft_plan(n_tok)
    g_by_n2 = g.reshape(n1, n2, LANES).transpose(1, 0, 2)
    tcol = _pick(c, 2048, LANES)
    ncb = c // tcol
    tab = pl.BlockSpec((k1p, n2), lambda j, i: (0, 0))
    out = pl.BlockSpec((k1p, tcol), lambda j, i: (0, i * ncb + j))
    return pl.pallas_call(
        functools.partial(_filter_taps_fft1_kernel, n_tok=n_tok, n2_size=n2, k1p=k1p),
        grid=(ncb, n2),
        in_specs=[
            pl.BlockSpec((None, n1, LANES), lambda j, i: (i, 0, 0)),
            pl.BlockSpec((3 * LANES, tcol), lambda j, i: (0, j)),
            pl.BlockSpec((3 * LANES, tcol), lambda j, i: (0, ncb + j)),
            pl.BlockSpec((1, tcol), lambda j, i: (0, j)),
            pl.BlockSpec((2 * k1p, 3 * n1), lambda j, i: (0, 0)), tab, tab,
        ],
        out_specs=[out, out, pl.BlockSpec((1, tcol), lambda j, i: (0, j))],
        out_shape=[jax.ShapeDtypeStruct((k1p, n2 * c), F32)] * 2 + [jax.ShapeDtypeStruct((1, c), F32)],
        compiler_params=_params("parallel", "arbitrary"),
        name="hyena_filter_fft1",
    )(g_by_n2, w3, w3, deltas, consts["f1_full"], consts["tw_cos"], consts["tw_sin"])


def _split3(x):
    hi = x.astype(BF16)
    lo = (x - hi.astype(F32)).astype(BF16)
    return jnp.concatenate([hi, lo, hi], axis=0)


def _mat3(m64):
    m = jnp.asarray(m64, F32)
    hi = m.astype(BF16)
    lo = (m - hi.astype(F32)).astype(BF16)
    return jnp.concatenate([hi, hi, lo], axis=1)


def _dft_plan(n_tok):
    n = 2 * n_tok
    n1 = 1 << ((n.bit_length() - 1 + 1) // 2)
    n2 = n // n1
    assert n1 * n2 == n and n2 >= SUBLANES, (n_tok, n1, n2)
    k1 = n1 // 2 + 1
    k1p = -(-k1 // BF16_ROWS) * BF16_ROWS
    return n, n1, n2, k1, k1p


def _dft_constants(n_tok):
    n, n1, n2, k1, k1p = _dft_plan(n_tok)
    two_pi = 2.0 * np.pi
    kk = np.arange(k1p)[:, None]
    valid = (kk < k1).astype(np.float64)
    ang = two_pi * ((kk * np.arange(n1)[None, :]) % n1) / n1
    f1 = np.concatenate([np.cos(ang) * valid, -np.sin(ang) * valid], axis=0)
    a2 = two_pi * ((np.arange(n2)[:, None] * np.arange(n2)[None, :]) % n2) / n2
    c2, s2 = np.cos(a2), np.sin(a2)
    f2 = np.block([[c2, s2], [-s2, c2]])
    g2 = np.block([[c2, -s2], [s2, c2]])
    wk = np.where((np.arange(k1p) == 0) | (np.arange(k1p) == n1 // 2), 1.0, 2.0) * (np.arange(k1p) < k1)
    a3 = two_pi * ((np.arange(n1 // 2)[:, None] * np.arange(k1p)[None, :]) % n1) / n1
    g3 = np.concatenate([np.cos(a3) * wk[None, :], -np.sin(a3) * wk[None, :]], axis=1) / n
    at = two_pi * (kk * np.arange(n2)[None, :]) / n
    return dict(
        f1_full=_mat3(f1), f1_half=_mat3(f1[:, :n1 // 2]), f2=_mat3(f2), g2=_mat3(g2), g3=_mat3(g3),
        tw_cos=jnp.asarray(np.cos(at), F32), tw_sin=jnp.asarray(np.sin(at), F32),
        tw_cos_t=jnp.asarray(np.cos(at).T, F32), tw_sin_t=jnp.asarray(np.sin(at).T, F32),
    )


def _table_column(tab_ref, idx):
    lane = lax.broadcasted_iota(jnp.int32, tab_ref.shape, 1)
    return jnp.sum(jnp.where(lane == idx, tab_ref[...], 0.0), axis=1, keepdims=True)


def _fft1_kernel(x_ref, f_ref, tc_ref, ts_ref, are_ref, aim_ref, *, k1p, blocks_per_n2):
    n2 = pl.program_id(0) // blocks_per_n2
    a = jnp.dot(f_ref[...], _split3(x_ref[...]), preferred_element_type=F32)
    re, im = a[:k1p], a[k1p:]
    c = _table_column(tc_ref, n2)
    s = _table_column(ts_ref, n2)
    are_ref[...] = re * c + im * s
    aim_ref[...] = im * c - re * s


def _fft_stage1(x2d, f1, consts, n_tok, n_ch):
    _, n1, n2, _, k1p = _dft_plan(n_tok)
    n1c, cols = x2d.shape
    tcol = _pick(n_ch, 2048, LANES)
    tab = pl.BlockSpec((k1p, n2), lambda j: (0, 0))
    out = pl.BlockSpec((k1p, tcol), lambda j: (0, j))
    return pl.pallas_call(
        functools.partial(_fft1_kernel, k1p=k1p, blocks_per_n2=n_ch // tcol),
        grid=(cols // tcol,),
        in_specs=[pl.BlockSpec((n1c, tcol), lambda j: (0, j)),
                  pl.BlockSpec((2 * k1p, 3 * n1c), lambda j: (0, 0)), tab, tab],
        out_specs=[out, out],
        out_shape=[jax.ShapeDtypeStruct((k1p, cols), F32)] * 2,
        compiler_params=_params("parallel"),
        name="hyena_fft1",
    )(x2d, f1, consts["tw_cos"], consts["tw_sin"])


def _fft2_spec_kernel(are_ref, aim_ref, f_ref, asum_ref, sre_ref, sim_ref, *, tk1, n2):
    inv = 1.0 / (asum_ref[...] + NORM_EPS)

    def forward(r):
        a = jnp.concatenate([are_ref[r], aim_ref[r]], axis=0)
        return jnp.dot(f_ref[...], _split3(a), preferred_element_type=F32)

    def store(r, x):
        sre_ref[r] = x[:n2] * inv
        sim_ref[r] = x[n2:] * inv

    xs = {}
    for r in range(tk1 + DFT_ROW_LAG):
        if r < tk1:
            xs[r] = forward(r)
        if r >= DFT_ROW_LAG:
            store(r - DFT_ROW_LAG, xs.pop(r - DFT_ROW_LAG))


def _fft2_conv_kernel(are_ref, aim_ref, sre_ref, sim_ref, f_ref, g_ref, tc_ref, ts_ref,
                      bre_ref, bim_ref, *, tk1, n2):
    i = pl.program_id(0)

    def forward(r):
        a = jnp.concatenate([are_ref[r], aim_ref[r]], axis=0)
        return jnp.dot(f_ref[...], _split3(a), preferred_element_type=F32)

    def inverse(r, x):
        xr, xi = x[:n2], x[n2:]
        sr, si = sre_ref[r], sim_ref[r]
        y = jnp.concatenate([xr * sr - xi * si, xr * si + xi * sr], axis=0)
        return jnp.dot(g_ref[...], _split3(y), preferred_element_type=F32)

    def store(r, b):
        br, bi = b[:n2], b[n2:]
        c = _table_column(tc_ref, i * tk1 + r)
        s = _table_column(ts_ref, i * tk1 + r)
        bre_ref[r] = br * c - bi * s
        bim_ref[r] = bi * c + br * s

    xs, bs = {}, {}
    for r in range(tk1 + 2 * DFT_ROW_LAG):
        if r < tk1:
            xs[r] = forward(r)
        if 0 <= r - DFT_ROW_LAG < tk1:
            bs[r - DFT_ROW_LAG] = inverse(r - DFT_ROW_LAG, xs.pop(r - DFT_ROW_LAG))
        if 0 <= r - 2 * DFT_ROW_LAG < tk1:
            store(r - 2 * DFT_ROW_LAG, bs.pop(r - 2 * DFT_ROW_LAG))


def _fft_stage2(are, aim, consts, n_tok, n_ch, *, spec=None, asum=None, spec_offset=0):
    _, n1, n2, _, k1p = _dft_plan(n_tok)
    tk1 = SUBLANES
    td = _pick(n_ch, 512, LANES)
    blk = pl.BlockSpec((tk1, n2, td), lambda i, j: (i, 0, j))
    mat = pl.BlockSpec((2 * n2, 6 * n2), lambda i, j: (0, 0))
    grid = (k1p // tk1, n_ch // td)
    out_shape = [jax.ShapeDtypeStruct((k1p, n2, n_ch), F32)] * 2
    if spec is None:
        return pl.pallas_call(
            functools.partial(_fft2_spec_kernel, tk1=tk1, n2=n2),
            grid=grid,
            in_specs=[blk, blk, mat, pl.BlockSpec((1, td), lambda i, j: (0, j))],
            out_specs=[blk, blk],
            out_shape=out_shape,
            compiler_params=_params("parallel", "parallel"),
            name="hyena_fft2_spec",
        )(are, aim, consts["f2"], asum)
    off = spec_offset // td
    sblk = pl.BlockSpec((tk1, n2, td), lambda i, j: (i, 0, j + off))
    tab = pl.BlockSpec((n2, k1p), lambda i, j: (0, 0))
    return pl.pallas_call(
        functools.partial(_fft2_conv_kernel, tk1=tk1, n2=n2),
        grid=grid,
        in_specs=[blk, blk, sblk, sblk, mat, mat, tab, tab],
        out_specs=[blk, blk],
        out_shape=out_shape,
        compiler_params=_params("parallel", "parallel"),
        name="hyena_fft2_conv",
    )(are, aim, spec[0], spec[1], consts["f2"], consts["g2"], consts["tw_cos_t"], consts["tw_sin_t"])


def _fft3_kernel(bre_ref, bim_ref, g_ref, y_ref, gate_ref, bias_ref, o_ref):
    b = jnp.concatenate([bre_ref[...], bim_ref[...]], axis=0)
    conv = jnp.dot(g_ref[...], _split3(b), preferred_element_type=F32)
    o_ref[...] = (gate_ref[...] * (conv + y_ref[...] * bias_ref[...])).astype(o_ref.dtype)


def _fft_stage3(bre, bim, consts, n_tok, n_ch, y2d, gate2d, bias, out_dtype):
    _, n1, n2, _, k1p = _dft_plan(n_tok)
    cols = n2 * n_ch
    tcol = _pick(n_ch, 2048, LANES)
    per = n_ch // tcol
    inb = pl.BlockSpec((k1p, tcol), lambda j: (0, j))
    sig = pl.BlockSpec((n1 // 2, tcol), lambda j: (0, j))
    return pl.pallas_call(
        _fft3_kernel,
        grid=(cols // tcol,),
        in_specs=[inb, inb, pl.BlockSpec((n1 // 2, 6 * k1p), lambda j: (0, 0)), sig, sig,
                  pl.BlockSpec((1, tcol), lambda j: (0, j % per))],
        out_specs=sig,
        out_shape=jax.ShapeDtypeStruct((n1 // 2, cols), out_dtype),
        compiler_params=_params("parallel"),
        name="hyena_fft3",
    )(bre, bim, consts["g3"], y2d, gate2d, bias.reshape(1, n_ch))


def _hyena_mixer_core(h, w_in, b_in, conv_w, conv_b, f_w1, f_b1, f_w2, f_b2, f_w3, f_freq, f_bias):
    n_tok, d = h.shape
    _, n1, n2, _, k1p = _dft_plan(n_tok)
    consts = _dft_constants(n_tok)
    z = _matmul(h, (w_in,), "bias", F32, extra=(b_in,))
    zs = _short_conv(z, conv_w, conv_b, HY_ORDER + 1)

    c = HY_ORDER * d
    tre, tim, asum = _hyena_filter_stage1(n_tok, d, consts, f_w1, f_b1, f_w2, f_b2, f_w3, f_freq)
    spec = _fft_stage2(tre.reshape(k1p, n2, c), tim.reshape(k1p, n2, c), consts, n_tok, c, asum=asum)

    y = zs[0].reshape(n1 // 2, n2 * d)
    for n in range(HY_ORDER):
        are, aim = _fft_stage1(y, consts["f1_half"], consts, n_tok, d)
        bre, bim = _fft_stage2(are.reshape(k1p, n2, d), aim.reshape(k1p, n2, d), consts, n_tok, d,
                               spec=spec, spec_offset=n * d)
        last = n == HY_ORDER - 1
        y = _fft_stage3(bre.reshape(k1p, n2 * d), bim.reshape(k1p, n2 * d), consts, n_tok, d, y,
                        zs[n + 1].reshape(n1 // 2, n2 * d), f_bias[n], BF16 if last else F32)
    return y.reshape(n_tok, d)


def _axial_rope_tables(n_tok):
    rows = n_tok // GRID_W
    row = jnp.repeat(jnp.arange(rows, dtype=F32), GRID_W)
    col = (jnp.arange(rows * GRID_W) % GRID_W).astype(F32)
    n_pair = HEAD_DIM // 4
    inv_freq = ROPE_THETA ** (-jnp.arange(n_pair, dtype=F32) / n_pair)
    ang = jnp.concatenate([row[:, None] * inv_freq, col[:, None] * inv_freq], axis=-1)
    cos, sin = jnp.cos(ang), jnp.sin(ang)
    signed_sin = jnp.stack([-sin, sin], axis=-1).reshape(sin.shape[0], HEAD_DIM)
    return jnp.repeat(cos, 2, axis=-1), signed_sin


def kernel(x, c, ctx, c_ctx, ada_a, ada_b, ada_bias, norm1_g, norm2_g, ffn_w_gate, ffn_w_up, ffn_w_down,
           hy_w_in, hy_b_in, hy_conv_w, hy_conv_b, hy_f_w1, hy_f_b1, hy_f_w2, hy_f_b2, hy_f_w3, hy_f_freq,
           hy_f_bias, hy_w_out, gqa_w_q, gqa_w_k, gqa_w_v, gqa_q_norm, gqa_k_norm, gqa_w_o,
           diff_w_q, diff_w_k, diff_w_v, diff_q_norm, diff_k_norm, diff_lambda_q1, diff_lambda_k1,
           diff_lambda_q2, diff_lambda_k2, diff_sub_norm, diff_w_o):
    assert x.shape[0] == 1 and ctx.shape[0] == 1
    depth = ada_a.shape[0]
    n_tok, d = x.shape[1], x.shape[2]
    xs = x[0]
    cs = ctx[0]
    cos, sin = _axial_rope_tables(n_tok)
    n_ctx = ctx.shape[1]
    cos_all = jnp.concatenate([cos, jnp.ones((n_ctx, HEAD_DIM), F32)], axis=0)
    sin_all = jnp.concatenate([sin, jnp.zeros((n_ctx, HEAD_DIM), F32)], axis=0)

    def keys_values(h, hc, w_k, w_v, k_gain, head_width):
        h_all = jnp.concatenate([h, hc], axis=0)
        t = h_all.shape[0]
        tm = _pick(t, 1280, MM_ROW_CHUNK) if t % MM_ROW_CHUNK == 0 else t
        k = _matmul(h_all, (w_k,), "headrope", BF16, extra=(k_gain, cos_all, sin_all), tm=tm)
        vt = _matmul(h_all, (w_v,), "values_t", BF16, tm=tm, head_width=head_width)
        return k, vt

    cc = jnp.concatenate([c.reshape(1, d), c_ctx.reshape(1, d), jnp.zeros((SUBLANES - 2, d), F32)], axis=0)
    mods = _ada_mods(cc, ada_a, ada_b, ada_bias)

    kinds = [i % N_MIXERS for i in range(depth)]
    for i in range(depth):
        kind = kinds[i]
        j = i // N_MIXERS
        reads_ctx = kind != MIX_HYENA
        update_ctx = any(kk != MIX_HYENA for kk in kinds[i + 1:])
        sh1, sc1, g1, sh2, sc2, g2 = [mods[i, 0, m * d:(m + 1) * d] for m in range(N_MOD)]
        csh1, csc1, cg1, csh2, csc2, cg2 = [mods[i, 1, m * d:(m + 1) * d] for m in range(N_MOD)]
        h = _norm_mod(xs, norm1_g[i], sh1, sc1)
        hc = _norm_mod(cs, norm1_g[i], csh1, csc1) if (reads_ctx or update_ctx) else None
        yc = None
        if kind == MIX_HYENA:
            w_in = hy_w_in[j].astype(BF16)
            w_out = hy_w_out[j].astype(BF16)
            args = (w_in, hy_b_in[j], hy_conv_w[j], hy_conv_b[j], hy_f_w1[j], hy_f_b1[j], hy_f_w2[j],
                    hy_f_b2[j], hy_f_w3[j], hy_f_freq[j], hy_f_bias[j])
            y = _hyena_mixer_core(h, *args)
            xs = _matmul(y, (w_out,), "resid", F32, extra=(xs, g1))
            if update_ctx:
                yc = _hyena_mixer_core(hc, *args)
                cs = _matmul(yc, (w_out,), "resid", F32, extra=(cs, cg1))
        elif kind == MIX_GQA:
            w_q = gqa_w_q[j].astype(BF16)
            w_k = gqa_w_k[j].astype(BF16)
            w_v = gqa_w_v[j].astype(BF16)
            w_o = gqa_w_o[j].astype(BF16)
            qg, kg = gqa_q_norm[j], gqa_k_norm[j]
            q = _matmul(h, (w_q,), "headrope", BF16, extra=(qg, cos, sin), out_scale=ATTN_Q_SCALE)
            k, vt = keys_values(h, hc, w_k, w_v, kg, HEAD_DIM)
            o = _attention(q, k, vt, "gqa")
            xs = _matmul(o, (w_o,), "resid", F32, extra=(xs, g1))
            if update_ctx:
                qc = _matmul(hc, (w_q,), "headnorm", BF16, extra=(qg,), out_scale=ATTN_Q_SCALE)
                oc = _attention(qc, k[n_tok:], vt[:, :, n_tok:], "gqa")
                cs = _matmul(oc, (w_o,), "resid", F32, extra=(cs, cg1))
        else:
            w_q = diff_w_q[j].astype(BF16)
            w_k = diff_w_k[j].astype(BF16)
            w_v = diff_w_v[j].astype(BF16)
            w_o = diff_w_o[j].astype(BF16)
            qg, kg = diff_q_norm[j], diff_k_norm[j]
            lam_init = 0.8 - 0.6 * math.exp(-0.3 * i)
            lam_params = (diff_lambda_q1[j], diff_lambda_k1[j], diff_lambda_q2[j], diff_lambda_k2[j])
            attend = functools.partial(_attention, kind="diff", lam_params=lam_params,
                                       sub_gain=diff_sub_norm[j], lam_init=lam_init)
            q = _matmul(h, (w_q,), "headrope", BF16, extra=(qg, cos, sin), out_scale=ATTN_Q_SCALE)
            k, vt = keys_values(h, hc, w_k, w_v, kg, 2 * HEAD_DIM)
            o = attend(q, k, vt)
            xs = _matmul(o, (w_o,), "resid", F32, extra=(xs, g1))
            if update_ctx:
                qc = _matmul(hc, (w_q,), "headnorm", BF16, extra=(qg,), out_scale=ATTN_Q_SCALE)
                oc = attend(qc, k[n_tok:], vt[:, :, n_tok:])
                cs = _matmul(oc, (w_o,), "resid", F32, extra=(cs, cg1))

        w_gate = ffn_w_gate[i].astype(BF16)
        w_up = ffn_w_up[i].astype(BF16)
        w_down = ffn_w_down[i].astype(BF16)
        h2 = _norm_mod(xs, norm2_g[i], sh2, sc2)
        act = _matmul(h2, (w_gate, w_up), "swiglu", BF16, tn=256)
        xs = _matmul(act, (w_down,), "resid", F32, extra=(xs, g2), tm=512, tn=256)
        if update_ctx:
            hc2 = _norm_mod(cs, norm2_g[i], csh2, csc2)
            actc = _matmul(hc2, (w_gate, w_up), "swiglu", BF16, tn=256)
            cs = _matmul(actc, (w_down,), "resid", F32, extra=(cs, cg2), tm=512, tn=256)
    return xs[None]
```
